```python
import math
import jax, jax.numpy as jnp
from jax import lax
import numpy as np

D_MODEL = 4096
BATCH = 4
SEQ = 2048
DEPTH = 2

GRID_W = 64
CTX_LEN = 256
HEAD_DIM = 128
DIFF_WIDTH = D_MODEL // 4
DIFF_HEADS = DIFF_WIDTH // HEAD_DIM
DIFF_QK_DIM = HEAD_DIM // 2
GQA_WIDTH = D_MODEL // 4
GQA_HEADS = GQA_WIDTH // HEAD_DIM
GQA_KV_HEADS = 2
GQA_GROUP = GQA_HEADS // GQA_KV_HEADS
LRU_WIDTH = D_MODEL // 2
LRU_BLOCK = 128
LRU_BLOCKS = LRU_WIDTH // LRU_BLOCK
CONV_WIDTH = 4
CONV_PAD_LEFT = (CONV_WIDTH - 1) // 2
LRU_C = 8.0
MIX_WIDTH = DIFF_WIDTH + GQA_WIDTH + LRU_WIDTH
D_FF = 4 * D_MODEL
PROJ_SIZES = (DIFF_WIDTH, DIFF_WIDTH, DIFF_WIDTH,
              GQA_WIDTH, GQA_KV_HEADS * HEAD_DIM, GQA_KV_HEADS * HEAD_DIM,
              LRU_WIDTH, LRU_WIDTH)
IN_COLS = sum(PROJ_SIZES)
Q_BLOCK = 128
ROPE_THETA = 10000.0
EPS = 1e-6
N_MOD = 6

kernel_name = 'hymba_style_diffusion_hybrid'


def rms_norm(x, gain=None):
    xf = x.astype(jnp.float32)
    y = xf * lax.rsqrt(jnp.mean(xf * xf, axis=-1, keepdims=True) + EPS)
    if gain is not None:
        y = y * gain.astype(jnp.float32)
    return y.astype(x.dtype)


def modulate(h, shift, scale):
    return h * (1 + scale) + shift


def grid_angles(n, dim):
    rows = n // GRID_W
    row = jnp.repeat(jnp.arange(rows, dtype=jnp.float32), GRID_W, total_repeat_length=n)
    col = jnp.tile(jnp.arange(GRID_W, dtype=jnp.float32), rows)
    n_freq = dim // 4
    inv = ROPE_THETA ** (-jnp.arange(n_freq, dtype=jnp.float32) / n_freq)
    ang = jnp.stack([row[:, None] * inv, col[:, None] * inv], axis=1)
    return jnp.cos(ang), jnp.sin(ang)


def apply_rope_2d(x, cos, sin):
    b, n, h, d = x.shape
    xr = x.astype(jnp.float32).reshape(b, n, h, 2, 2, d // 4)
    x1, x2 = xr[..., 0, :], xr[..., 1, :]
    cs = cos[None, :, None]
    sn = sin[None, :, None]
    out = jnp.stack([x1 * cs - x2 * sn, x2 * cs + x1 * sn], axis=-2)
    return out.reshape(b, n, h, d).astype(x.dtype)


def split_projection(p):
    idx = np.cumsum(PROJ_SIZES)[:-1].tolist()
    return jnp.split(p, idx, axis=-1)


def sweep_query_blocks(fn, q):
    b, n = q.shape[:2]
    nb = n // Q_BLOCK
    qb = jnp.moveaxis(q.reshape(b, nb, Q_BLOCK, *q.shape[2:]), 1, 0)
    out = lax.map(fn, qb)
    return jnp.moveaxis(out, 0, 1).reshape(b, n, *out.shape[3:])


def diff_qk_heads(t, rope):
    b, n, _ = t.shape
    t = t.reshape(b, n, 2 * DIFF_HEADS, DIFF_QK_DIM)
    if rope is not None:
        t = apply_rope_2d(t, *rope)
    return t.reshape(b, n, DIFF_HEADS, 2, DIFF_QK_DIM)


def diff_attention(q, k, v, lam):
    def block(qb):
        s = jnp.einsum('bqhcd,bkhcd->bhcqk', qb, k, preferred_element_type=jnp.float32)
        p = jax.nn.softmax(s, axis=-1)
        w = p[:, :, 0] - lam * p[:, :, 1]
        return jnp.einsum('bhqk,bkhe->bqhe', w.astype(v.dtype), v)
    return sweep_query_blocks(block, q)


def diff_output(o, gain, lam_init):
    b, n = o.shape[:2]
    return (rms_norm(o, gain) * (1 - lam_init)).reshape(b, n, DIFF_WIDTH)


def gqa_heads(t, n_heads, gain, rope):
    b, n, _ = t.shape
    t = rms_norm(t.reshape(b, n, n_heads, HEAD_DIM), gain)
    if rope is not None:
        t = apply_rope_2d(t, *rope)
    return t


def gqa_attention(q, k, v):
    def block(qb):
        s = jnp.einsum('bqkgd,bskd->bkgqs', qb, k, preferred_element_type=jnp.float32)
        p = jax.nn.softmax(s, axis=-1).astype(v.dtype)
        return jnp.einsum('bkgqs,bskd->bqkgd', p, v)
    return sweep_query_blocks(block, q)


def depthwise_conv_centred(x, w, bias):
    ch = x.shape[-1]
    y = lax.conv_general_dilated(
        x, w[:, None, :].astype(x.dtype), window_strides=(1,),
        padding=[(CONV_PAD_LEFT, CONV_WIDTH - 1 - CONV_PAD_LEFT)],
        dimension_numbers=('NWC', 'WIO', 'NWC'), feature_group_count=ch)
    return y + bias


def rglru_coeffs(x, w_a, b_a, w_i, b_i, a_logit):
    b, n, _ = x.shape
    xf = x.astype(jnp.float32)
    xb = xf.reshape(b, n, LRU_BLOCKS, LRU_BLOCK)
    r = jax.nn.sigmoid(jnp.einsum('bngi,dgij->dbngj', xb, w_a.astype(jnp.float32)).reshape(2, b, n, LRU_WIDTH)
                       + b_a.astype(jnp.float32)[:, None, None])
    i = jax.nn.sigmoid(jnp.einsum('bngi,dgij->dbngj', xb, w_i.astype(jnp.float32)).reshape(2, b, n, LRU_WIDTH)
                       + b_i.astype(jnp.float32)[:, None, None])
    log_a = -LRU_C * r * jax.nn.softplus(-a_logit.astype(jnp.float32))[:, None, None]
    a = jnp.exp(log_a)
    u = jnp.sqrt(-jnp.expm1(2.0 * log_a)) * (i * xf[None])
    return a, u


def linear_scan(a, u, h0, reverse):
    def combine(e1, e2):
        a1, u1 = e1
        a2, u2 = e2
        return a1 * a2, a2 * u1 + u2
    a_cum, u_cum = lax.associative_scan(combine, (a, u), reverse=reverse, axis=1)
    return a_cum * h0[:, None] + u_cum


def rglru_mixer(x_lat, g_lat, x_ctx, g_ctx, conv_w, conv_b, w_a, b_a, w_i, b_i, a_logit, need_ctx):
    b = x_lat.shape[0]
    a_c, u_c = rglru_coeffs(depthwise_conv_centred(x_ctx, conv_w, conv_b), w_a, b_a, w_i, b_i, a_logit)
    a_l, u_l = rglru_coeffs(depthwise_conv_centred(x_lat, conv_w, conv_b), w_a, b_a, w_i, b_i, a_logit)
    zero = jnp.zeros((b, LRU_WIDTH), jnp.float32)
    hc_f = linear_scan(a_c[0], u_c[0], zero, False)
    hc_b = linear_scan(a_c[1], u_c[1], zero, True)
    hl_f = linear_scan(a_l[0], u_l[0], hc_f[:, -1], False)
    hl_b = linear_scan(a_l[1], u_l[1], hc_b[:, 0], True)
    y_lat = (hl_f + hl_b).astype(g_lat.dtype) * jax.nn.gelu(g_lat)
    if not need_ctx:
        return y_lat, None
    y_ctx = (hc_f + hc_b).astype(g_ctx.dtype) * jax.nn.gelu(g_ctx)
    return y_lat, y_ctx


def token_mixer(h_lat, h_ctx, w_in, lam_vecs, subln, q_gain, k_gain, conv_w, conv_b,
                w_a, b_a, w_i, b_i, a_logit, w_out, rope_diff, rope_gqa, layer_idx, need_ctx):
    b, n, _ = h_lat.shape
    m = h_ctx.shape[1]
    dq_l, dk_l, dv_l, gq_l, gk_l, gv_l, rx_l, rg_l = split_projection(h_lat @ w_in)
    dq_c, dk_c, dv_c, gq_c, gk_c, gv_c, rx_c, rg_c = split_projection(h_ctx @ w_in)

    lam_init = 0.8 - 0.6 * math.exp(-0.3 * layer_idx)
    lv = lam_vecs.astype(jnp.float32)
    lam = jnp.exp(jnp.sum(lv[0] * lv[1])) - jnp.exp(jnp.sum(lv[2] * lv[3])) + lam_init
    d_scale = DIFF_QK_DIM ** -0.5
    dk_ctx = diff_qk_heads(dk_c, None)
    dv_ctx = dv_c.reshape(b, m, DIFF_HEADS, HEAD_DIM)
    dk_all = jnp.concatenate([dk_ctx, diff_qk_heads(dk_l, rope_diff)], axis=1)
    dv_all = jnp.concatenate([dv_ctx, dv_l.reshape(b, n, DIFF_HEADS, HEAD_DIM)], axis=1)
    diff_lat = diff_output(diff_attention(diff_qk_heads(dq_l, rope_diff) * d_scale, dk_all, dv_all, lam),
                           subln, lam_init)

    g_scale = HEAD_DIM ** -0.5
    gk_ctx = gqa_heads(gk_c, GQA_KV_HEADS, k_gain, None)
    gv_ctx = gv_c.reshape(b, m, GQA_KV_HEADS, HEAD_DIM)
    gk_all = jnp.concatenate([gk_ctx, gqa_heads(gk_l, GQA_KV_HEADS, k_gain, rope_gqa)], axis=1)
    gv_all = jnp.concatenate([gv_ctx, gv_l.reshape(b, n, GQA_KV_HEADS, HEAD_DIM)], axis=1)
    gq_lat = gqa_heads(gq_l, GQA_HEADS, q_gain, rope_gqa).reshape(b, n, GQA_KV_HEADS, GQA_GROUP, HEAD_DIM) * g_scale
    gqa_lat = gqa_attention(gq_lat, gk_all, gv_all).reshape(b, n, GQA_WIDTH)

    lru_lat, lru_ctx = rglru_mixer(rx_l, rg_l, rx_c, rg_c, conv_w, conv_b, w_a, b_a, w_i, b_i, a_logit, need_ctx)

    y_lat = jnp.concatenate([diff_lat, gqa_lat, lru_lat], axis=-1) @ w_out
    if not need_ctx:
        return y_lat, None
    diff_ctx = diff_output(diff_attention(diff_qk_heads(dq_c, None) * d_scale, dk_ctx, dv_ctx, lam), subln, lam_init)
    gq_ctx = gqa_heads(gq_c, GQA_HEADS, q_gain, None).reshape(b, m, GQA_KV_HEADS, GQA_GROUP, HEAD_DIM) * g_scale
    gqa_ctx = gqa_attention(gq_ctx, gk_ctx, gv_ctx).reshape(b, m, GQA_WIDTH)
    y_ctx = jnp.concatenate([diff_ctx, gqa_ctx, lru_ctx], axis=-1) @ w_out
    return y_lat, y_ctx


def sq_relu_mlp(h, w1, w2):
    return jnp.square(jax.nn.relu(h @ w1)) @ w2


def setup_inputs(seed: int = 0) -> dict:
    key = jax.random.key(seed)
    ks = jax.random.split(key, 24)
    f32 = jnp.float32

    def nrm(k, shape, scale):
        return jax.random.normal(k, shape, f32) * scale

    L = DEPTH
    u = jax.random.uniform(ks[17], (L, 2, LRU_WIDTH), f32, 0.9, 0.999)
    s = u ** (1.0 / LRU_C)
    return {
        'x': nrm(ks[0], (BATCH, SEQ, D_MODEL), 1.0),
        'c': nrm(ks[1], (BATCH, D_MODEL), 1.0),
        'ctx': nrm(ks[2], (BATCH, CTX_LEN, D_MODEL), 1.0),
        'c_ctx': nrm(ks[3], (D_MODEL,), 1.0),
        'w_mod': nrm(ks[4], (L, D_MODEL, N_MOD * D_MODEL), 0.5 * D_MODEL ** -0.5),
        'b_mod': nrm(ks[5], (L, N_MOD * D_MODEL), 0.01),
        'w_in': nrm(ks[6], (L, D_MODEL, IN_COLS), D_MODEL ** -0.5),
        'diff_lambda': nrm(ks[7], (L, 4, DIFF_QK_DIM), 0.1),
        'diff_subln': 1.0 + nrm(ks[8], (L, HEAD_DIM), 0.01),
        'gqa_q_norm': 1.0 + nrm(ks[9], (L, HEAD_DIM), 0.01),
        'gqa_k_norm': 1.0 + nrm(ks[10], (L, HEAD_DIM), 0.01),
        'lru_conv_w': nrm(ks[11], (L, CONV_WIDTH, LRU_WIDTH), CONV_WIDTH ** -0.5),
        'lru_conv_b': nrm(ks[12], (L, LRU_WIDTH), 0.01),
        'lru_w_a': nrm(ks[13], (L, 2, LRU_BLOCKS, LRU_BLOCK, LRU_BLOCK), LRU_BLOCK ** -0.5),
        'lru_b_a': nrm(ks[14], (L, 2, LRU_WIDTH), 0.01),
        'lru_w_i': nrm(ks[15], (L, 2, LRU_BLOCKS, LRU_BLOCK, LRU_BLOCK), LRU_BLOCK ** -0.5),
        'lru_b_i': nrm(ks[16], (L, 2, LRU_WIDTH), 0.01),
        'lru_a_logit': jnp.log(s) - jnp.log1p(-s),
        'w_out': nrm(ks[18], (L, MIX_WIDTH, D_MODEL), MIX_WIDTH ** -0.5),
        'w_ff1': nrm(ks[19], (L, D_MODEL, D_FF), D_MODEL ** -0.5),
        'w_ff2': nrm(ks[20], (L, D_FF, D_MODEL), D_FF ** -0.5),
        'final_norm': 1.0 + nrm(ks[21], (D_MODEL,), 0.01),
    }


def reference(x, c, ctx, c_ctx, w_mod, b_mod, w_in, diff_lambda, diff_subln, gqa_q_norm, gqa_k_norm,
              lru_conv_w, lru_conv_b, lru_w_a, lru_b_a, lru_w_i, lru_b_i, lru_a_logit,
              w_out, w_ff1, w_ff2, final_norm):
    b, n, _ = x.shape
    rope_diff = grid_angles(n, DIFF_QK_DIM)
    rope_gqa = grid_angles(n, HEAD_DIM)
    xc = ctx
    for l in range(DEPTH):
        need_ctx = l < DEPTH - 1
        m_lat = jnp.moveaxis((jax.nn.silu(c) @ w_mod[l] + b_mod[l]).reshape(b, N_MOD, 1, D_MODEL), 1, 0)
        m_ctx = (jax.nn.silu(c_ctx) @ w_mod[l] + b_mod[l]).reshape(N_MOD, 1, 1, D_MODEL)
        sh1, sc1, g1, sh2, sc2, g2 = m_lat
        csh1, csc1, cg1, csh2, csc2, cg2 = m_ctx
        h_lat = modulate(rms_norm(x), sh1, sc1)
        h_ctx = modulate(rms_norm(xc), csh1, csc1)
        y_lat, y_ctx = token_mixer(h_lat, h_ctx, w_in[l], diff_lambda[l], diff_subln[l], gqa_q_norm[l], gqa_k_norm[l],
                                   lru_conv_w[l], lru_conv_b[l], lru_w_a[l], lru_b_a[l], lru_w_i[l], lru_b_i[l],
                                   lru_a_logit[l], w_out[l], rope_diff, rope_gqa, l, need_ctx)
        x = x + g1 * y_lat
        x = x + g2 * sq_relu_mlp(modulate(rms_norm(x), sh2, sc2), w_ff1[l], w_ff2[l])
        if need_ctx:
            xc = xc + cg1 * y_ctx
            xc = xc + cg2 * sq_relu_mlp(modulate(rms_norm(xc), csh2, csc2), w_ff1[l], w_ff2[l])
    return rms_norm(x, final_norm)
```

```python
import functools
import math

import jax
import jax.numpy as jnp
from jax import lax
from jax.experimental import pallas as pl
from jax.experimental.pallas import tpu as pltpu

D_MODEL = 4096
BATCH = 4
SEQ = 2048
DEPTH = 2
GRID_W = 64
CTX_LEN = 256
HEAD_DIM = 128
DIFF_WIDTH = D_MODEL // 4
DIFF_HEADS = DIFF_WIDTH // HEAD_DIM
DIFF_QK_DIM = HEAD_DIM // 2
GQA_WIDTH = D_MODEL // 4
GQA_HEADS = GQA_WIDTH // HEAD_DIM
GQA_KV_HEADS = 2
GQA_GROUP = GQA_HEADS // GQA_KV_HEADS
LRU_WIDTH = D_MODEL // 2
LRU_BLOCK = 128
LRU_BLOCKS = LRU_WIDTH // LRU_BLOCK
CONV_WIDTH = 4
LRU_C = 8.0
D_FF = 4 * D_MODEL
IN_COLS = 3 * DIFF_WIDTH + GQA_WIDTH + 2 * GQA_KV_HEADS * HEAD_DIM + 2 * LRU_WIDTH
ROPE_THETA = 10000.0
EPS = 1e-6
N_MOD = 6

N_LAT = BATCH * SEQ
N_CTX = BATCH * CTX_LEN
N_TOK = N_LAT + N_CTX
N_KEYS = CTX_LEN + SEQ

COL_DQ = 0
COL_DK = COL_DQ + DIFF_WIDTH
COL_DV = COL_DK + DIFF_WIDTH
COL_GQ = COL_DV + DIFF_WIDTH
COL_GK = COL_GQ + GQA_WIDTH
COL_GV = COL_GK + GQA_KV_HEADS * HEAD_DIM
COL_RX = COL_GV + GQA_KV_HEADS * HEAD_DIM
COL_RG = COL_RX + LRU_WIDTH

LANES = 128
VMEM_LIMIT = 56 * 1024 * 1024

F32 = jnp.float32
BF16 = jnp.bfloat16


def _params(n_axes, vmem=VMEM_LIMIT):
    return pltpu.CompilerParams(dimension_semantics=("arbitrary",) * n_axes, vmem_limit_bytes=vmem)


def _rms(x):
    return x * lax.rsqrt(jnp.mean(x * x, axis=-1, keepdims=True) + EPS)


def _mod_row(tile, tm):
    return jnp.minimum((tile * tm) // SEQ, BATCH)


def _mod_kernel(c_ref, w_ref, b_ref, o_ref):
    c = c_ref[...]
    s = (c * jax.nn.sigmoid(c)).astype(BF16)
    o_ref[...] = jnp.dot(s, w_ref[...].astype(BF16), preferred_element_type=F32) + b_ref[...]


def _modulation(c8, w_mod, b_mod):
    depth = w_mod.shape[0]
    tn = 512
    return pl.pallas_call(
        _mod_kernel,
        out_shape=jax.ShapeDtypeStruct((depth, 8, N_MOD * D_MODEL), F32),
        grid=(depth, N_MOD * D_MODEL // tn),
        in_specs=[
            pl.BlockSpec((8, D_MODEL), lambda l, j: (0, 0)),
            pl.BlockSpec((None, D_MODEL, tn), lambda l, j: (l, 0, j)),
            pl.BlockSpec((None, 1, tn), lambda l, j: (l, 0, j)),
        ],
        out_specs=pl.BlockSpec((None, 8, tn), lambda l, j: (l, 0, j)),
        compiler_params=_params(2),
        name="modulation",
    )(c8, w_mod, b_mod.reshape(depth, 1, N_MOD * D_MODEL))


def _mod_spec(layer, which, tm, tn, row_axis, col_axis):
    def index_map(*ids):
        row = _mod_row(ids[row_axis], tm)
        col = 0 if col_axis is None else ids[col_axis]
        return (layer * 8 * N_MOD + row * N_MOD + which, 0, col)
    return pl.BlockSpec((None, 1, tn), index_map)


def _norm_mod_kernel(x_ref, sh_ref, sc_ref, o_ref):
    y = _rms(x_ref[...])
    o_ref[...] = (y * (1.0 + sc_ref[...]) + sh_ref[...]).astype(BF16)


def _norm_mod(x, mods, layer):
    tm = 256
    return pl.pallas_call(
        _norm_mod_kernel,
        out_shape=jax.ShapeDtypeStruct((N_TOK, D_MODEL), BF16),
        grid=(N_TOK // tm,),
        in_specs=[
            pl.BlockSpec((tm, D_MODEL), lambda i: (i, 0)),
            _mod_spec(layer, 0, tm, D_MODEL, 0, None),
            _mod_spec(layer, 1, tm, D_MODEL, 0, None),
        ],
        out_specs=pl.BlockSpec((tm, D_MODEL), lambda i: (i, 0)),
        compiler_params=_params(1),
        name="norm_mod",
    )(x, mods, mods)


def _inproj_kernel(h_ref, w_ref, o_ref, wb_ref):
    @pl.when(pl.program_id(1) == 0)
    def _():
        wb_ref[...] = w_ref[...].astype(BF16)

    o_ref[...] = jnp.dot(h_ref[...], wb_ref[...], preferred_element_type=F32).astype(o_ref.dtype)


def _in_projection(h, w_in_l):
    tm, tn = 1024, 512
    return pl.pallas_call(
        _inproj_kernel,
        out_shape=jax.ShapeDtypeStruct((N_TOK, IN_COLS), BF16),
        grid=(IN_COLS // tn, N_TOK // tm),
        in_specs=[
            pl.BlockSpec((tm, D_MODEL), lambda j, i: (i, 0)),
            pl.BlockSpec((D_MODEL, tn), lambda j, i: (0, j)),
        ],
        out_specs=pl.BlockSpec((tm, tn), lambda j, i: (i, j)),
        scratch_shapes=[pltpu.VMEM((D_MODEL, tn), BF16)],
        compiler_params=_params(2),
        name="in_projection",
    )(h, w_in_l)


def _rope_tables(dim):
    n_freq = dim // 4
    pos = jnp.arange(SEQ)
    row = (pos // GRID_W).astype(F32)
    col = (pos % GRID_W).astype(F32)
    inv = ROPE_THETA ** (-jnp.arange(n_freq, dtype=F32) / n_freq)
    ang_r = row[:, None] * inv
    ang_c = col[:, None] * inv
    cos = jnp.concatenate([jnp.cos(ang_r), jnp.cos(ang_r), jnp.cos(ang_c), jnp.cos(ang_c)], axis=1)
    sin = jnp.concatenate([-jnp.sin(ang_r), jnp.sin(ang_r), -jnp.sin(ang_c), jnp.sin(ang_c)], axis=1)
    reps = LANES // dim
    cos = jnp.tile(cos, (1, reps))
    sin = jnp.tile(sin, (1, reps))
    cos = jnp.concatenate([cos, jnp.ones((CTX_LEN, LANES), F32)], axis=0)
    sin = jnp.concatenate([sin, jnp.zeros((CTX_LEN, LANES), F32)], axis=0)
    return cos, sin


def _rope(x, cos, sin, first, half):
    partner = jnp.where(first, pltpu.roll(x, LANES - half, 1), pltpu.roll(x, half, 1))
    return x * cos + partner * sin


def _qkprep_kernel(dq_ref, dk_ref, gq_ref, gk_ref, cd_ref, sd_ref, cg_ref, sg_ref, qgain_ref, kgain_ref,
                   qd0_ref, qd1_ref, kd_ref, qg_ref, kg_ref):
    cd, sd, cg, sg = cd_ref[...], sd_ref[...], cg_ref[...], sg_ref[...]
    qgain, kgain = qgain_ref[...], kgain_ref[...]
    lane = lax.broadcasted_iota(jnp.int32, cd.shape, 1)
    first_d = (lane & (DIFF_QK_DIM // 2 - 1)) < DIFF_QK_DIM // 4
    first_g = (lane & (HEAD_DIM // 2 - 1)) < HEAD_DIM // 4
    map0 = lane < DIFF_QK_DIM
    d_scale = DIFF_QK_DIM ** -0.5
    g_scale = HEAD_DIM ** -0.5
    for c in range(DIFF_WIDTH // LANES):
        sl = slice(c * LANES, (c + 1) * LANES)
        q = _rope(dq_ref[:, sl].astype(F32), cd, sd, first_d, DIFF_QK_DIM // 4) * d_scale
        qd0_ref[:, sl] = jnp.where(map0, q, 0.0).astype(BF16)
        qd1_ref[:, sl] = jnp.where(map0, 0.0, q).astype(BF16)
        kd_ref[:, sl] = _rope(dk_ref[:, sl].astype(F32), cd, sd, first_d, DIFF_QK_DIM // 4).astype(BF16)
    for c in range(GQA_WIDTH // LANES):
        sl = slice(c * LANES, (c + 1) * LANES)
        qn = _rms(gq_ref[:, sl].astype(F32)) * qgain
        qg_ref[:, sl] = (_rope(qn, cg, sg, first_g, HEAD_DIM // 4) * g_scale).astype(BF16)
    for c in range(GQA_KV_HEADS):
        sl = slice(c * LANES, (c + 1) * LANES)
        kn = _rms(gk_ref[:, sl].astype(F32)) * kgain
        kg_ref[:, sl] = _rope(kn, cg, sg, first_g, HEAD_DIM // 4).astype(BF16)


def _qk_prep(p, q_gain, k_gain, tables):
    tm = 256
    cos_d, sin_d, cos_g, sin_g = tables
    lat_tiles = N_LAT // tm

    def pos_block(i):
        return jnp.where(i < lat_tiles, i % (SEQ // tm), SEQ // tm)

    kv_w = GQA_KV_HEADS * HEAD_DIM
    tab_spec = pl.BlockSpec((tm, LANES), lambda i: (pos_block(i), 0))
    gain_spec = pl.BlockSpec((1, HEAD_DIM), lambda i: (0, 0))
    wide = lambda col: pl.BlockSpec((tm, DIFF_WIDTH), lambda i: (i, col // DIFF_WIDTH))
    return pl.pallas_call(
        _qkprep_kernel,
        out_shape=[
            jax.ShapeDtypeStruct((N_TOK, DIFF_WIDTH), BF16),
            jax.ShapeDtypeStruct((N_TOK, DIFF_WIDTH), BF16),
            jax.ShapeDtypeStruct((N_TOK, DIFF_WIDTH), BF16),
            jax.ShapeDtypeStruct((N_TOK, GQA_WIDTH), BF16),
            jax.ShapeDtypeStruct((N_TOK, kv_w), BF16),
        ],
        grid=(N_TOK // tm,),
        in_specs=[
            wide(COL_DQ), wide(COL_DK), wide(COL_GQ),
            pl.BlockSpec((tm, kv_w), lambda i: (i, COL_GK // kv_w)),
            tab_spec, tab_spec, tab_spec, tab_spec, gain_spec, gain_spec,
        ],
        out_specs=[
            pl.BlockSpec((tm, DIFF_WIDTH), lambda i: (i, 0)),
            pl.BlockSpec((tm, DIFF_WIDTH), lambda i: (i, 0)),
            pl.BlockSpec((tm, DIFF_WIDTH), lambda i: (i, 0)),
            pl.BlockSpec((tm, GQA_WIDTH), lambda i: (i, 0)),
            pl.BlockSpec((tm, kv_w), lambda i: (i, 0)),
        ],
        compiler_params=_params(1),
        name="qk_prep",
    )(p, p, p, p, cos_d, sin_d, cos_g, sin_g, q_gain.reshape(1, HEAD_DIM), k_gain.reshape(1, HEAD_DIM))


def _scores(qq, k_refs):
    return [lax.dot_general(qq, k[...], (((1,), (1,)), ((), ())), preferred_element_type=F32) for k in k_refs]


def _softmax_pv(ss, v_refs):
    m = ss[0].max(axis=-1, keepdims=True)
    for s in ss[1:]:
        m = jnp.maximum(m, s.max(axis=-1, keepdims=True))
    acc = None
    den = None
    for s, v in zip(ss, v_refs):
        e = jnp.exp(s - m)
        d = e.sum(axis=-1, keepdims=True)
        o = jnp.dot(e.astype(BF16), v[...], preferred_element_type=F32)
        acc = o if acc is None else acc + o
        den = d if den is None else den + d
    return acc, den


def _diff_attn_kernel(lam_ref, gain_ref, q0_ref, q1_ref, *refs, lam_init, n_blocks):
    k_refs = refs[0:2 * n_blocks:2]
    v_refs = refs[1:2 * n_blocks:2]
    o_ref = refs[2 * n_blocks]
    tq = q0_ref.shape[0]
    lv = lam_ref[...]
    lam = (jnp.exp(jnp.sum(lv[0:1] * lv[1:2], axis=-1, keepdims=True))
           - jnp.exp(jnp.sum(lv[2:3] * lv[3:4], axis=-1, keepdims=True)) + lam_init)
    qq = jnp.concatenate([q0_ref[...], q1_ref[...]], axis=0)
    acc, den = _softmax_pv(_scores(qq, k_refs), v_refs)
    o2 = acc / den
    o = o2[:tq] - lam * o2[tq:]
    o_ref[...] = (_rms(o) * gain_ref[...] * (1.0 - lam_init)).astype(o_ref.dtype)


def _diff_attention(qd0, qd1, kd, p, lam_vecs, subln, layer_idx, latent):
    lam_init = 0.8 - 0.6 * math.exp(-0.3 * layer_idx)
    v_col = COL_DV // HEAD_DIM
    ctx_blk = N_LAT // CTX_LEN
    if latent:
        tq = 256
        nq = SEQ // tq
        rows = N_LAT
        q_spec = pl.BlockSpec((tq, HEAD_DIM), lambda b, h, i: (b * nq + i, h))
        kv_specs = [
            pl.BlockSpec((CTX_LEN, HEAD_DIM), lambda b, h, i: (ctx_blk + b, h)),
            pl.BlockSpec((CTX_LEN, HEAD_DIM), lambda b, h, i: (ctx_blk + b, v_col + h)),
            pl.BlockSpec((SEQ, HEAD_DIM), lambda b, h, i: (b, h)),
            pl.BlockSpec((SEQ, HEAD_DIM), lambda b, h, i: (b, v_col + h)),
        ]
        kv_args = (kd, p, kd, p)
        out_spec = pl.BlockSpec((tq, HEAD_DIM), lambda b, h, i: (b * nq + i, h))
    else:
        tq = CTX_LEN
        nq = 1
        rows = N_CTX
        q_spec = pl.BlockSpec((tq, HEAD_DIM), lambda b, h, i: (ctx_blk + b, h))
        kv_specs = [
            pl.BlockSpec((CTX_LEN, HEAD_DIM), lambda b, h, i: (ctx_blk + b, h)),
            pl.BlockSpec((CTX_LEN, HEAD_DIM), lambda b, h, i: (ctx_blk + b, v_col + h)),
        ]
        kv_args = (kd, p)
        out_spec = pl.BlockSpec((tq, HEAD_DIM), lambda b, h, i: (b, h))
    return pl.pallas_call(
        functools.partial(_diff_attn_kernel, lam_init=lam_init, n_blocks=len(kv_args) // 2),
        out_shape=jax.ShapeDtypeStruct((rows, DIFF_WIDTH), BF16),
        grid=(BATCH, DIFF_HEADS, nq),
        in_specs=[
            pl.BlockSpec((4, DIFF_QK_DIM), lambda b, h, i: (0, 0)),
            pl.BlockSpec((1, HEAD_DIM), lambda b, h, i: (0, 0)),
            q_spec, q_spec, *kv_specs,
        ],
        out_specs=out_spec,
        compiler_params=_params(3),
        name="diff_attention_lat" if latent else "diff_attention_ctx",
    )(lam_vecs, subln.reshape(1, HEAD_DIM), qd0, qd1, *kv_args)


def _gqa_attn_kernel(q_ref, *refs, n_blocks):
    k_refs = refs[0:2 * n_blocks:2]
    v_refs = refs[1:2 * n_blocks:2]
    o_ref = refs[2 * n_blocks]
    tq = q_ref.shape[0]
    qq = jnp.concatenate([q_ref[:, g * HEAD_DIM:(g + 1) * HEAD_DIM] for g in range(GQA_GROUP)], axis=0)
    acc, den = _softmax_pv(_scores(qq, k_refs), v_refs)
    o = acc / den
    for g in range(GQA_GROUP):
        o_ref[:, g * HEAD_DIM:(g + 1) * HEAD_DIM] = o[g * tq:(g + 1) * tq].astype(o_ref.dtype)


def _gqa_attention(qg, kg, p, latent):
    v_col = COL_GV // HEAD_DIM
    ctx_blk = N_LAT // CTX_LEN
    gw = GQA_GROUP * HEAD_DIM
    if latent:
        tq = 128
        nq = SEQ // tq
        rows = N_LAT
        q_spec = pl.BlockSpec((tq, gw), lambda b, g, i: (b * nq + i, g))
        kv_specs = [
            pl.BlockSpec((CTX_LEN, HEAD_DIM), lambda b, g, i: (ctx_blk + b, g)),
            pl.BlockSpec((CTX_LEN, HEAD_DIM), lambda b, g, i: (ctx_blk + b, v_col + g)),
            pl.BlockSpec((SEQ, HEAD_DIM), lambda b, g, i: (b, g)),
            pl.BlockSpec((SEQ, HEAD_DIM), lambda b, g, i: (b, v_col + g)),
        ]
        kv_args = (kg, p, kg, p)
        out_spec = pl.BlockSpec((tq, gw), lambda b, g, i: (b * nq + i, g))
    else:
        tq = CTX_LEN
        nq = 1
        rows = N_CTX
        q_spec = pl.BlockSpec((tq, gw), lambda b, g, i: (ctx_blk + b, g))
        kv_specs = [
            pl.BlockSpec((CTX_LEN, HEAD_DIM), lambda b, g, i: (ctx_blk + b, g)),
            pl.BlockSpec((CTX_LEN, HEAD_DIM), lambda b, g, i: (ctx_blk + b, v_col + g)),
        ]
        kv_args = (kg, p)
        out_spec = pl.BlockSpec((tq, gw), lambda b, g, i: (b, g))
    return pl.pallas_call(
        functools.partial(_gqa_attn_kernel, n_blocks=len(kv_args) // 2),
        out_shape=jax.ShapeDtypeStruct((rows, GQA_WIDTH), BF16),
        grid=(BATCH, GQA_KV_HEADS, nq),
        in_specs=[q_spec, *kv_specs],
        out_specs=out_spec,
        compiler_params=_params(3),
        name="gqa_attention_lat" if latent else "gqa_attention_ctx",
    )(qg, *kv_args)


LRU_SLAB = 256
LRU_ROWS = 256


def _conv_centred(x, w, bias):
    n = x.shape[0]
    row = lax.broadcasted_iota(jnp.int32, x.shape, 0)
    xm1 = jnp.where(row >= 1, pltpu.roll(x, 1, 0), 0.0)
    xp1 = jnp.where(row < n - 1, pltpu.roll(x, n - 1, 0), 0.0)
    xp2 = jnp.where(row < n - 2, pltpu.roll(x, n - 2, 0), 0.0)
    return xm1 * w[0:1] + x * w[1:2] + xp1 * w[2:3] + xp2 * w[3:4] + bias


def _gelu_tanh(g):
    return 0.5 * g * (1.0 + jnp.tanh(0.7978845608028654 * (g + 0.044715 * (g * g * g))))


def _lru_kernel(*refs, need_ctx):
    if need_ctx:
        (xl_ref, xc_ref, gl_ref, gc_ref, cw_ref, cb_ref, wa_ref, ba_ref, wi_ref, bi_ref, al_ref,
         yl_ref, yc_ref, xconv_ref, a_ref, u_ref, hf_ref, hb_ref) = refs
    else:
        (xl_ref, xc_ref, gl_ref, cw_ref, cb_ref, wa_ref, ba_ref, wi_ref, bi_ref, al_ref,
         yl_ref, xconv_ref, a_ref, u_ref, hf_ref, hb_ref) = refs
    cw = cw_ref[...]
    cb = cb_ref[...]
    xconv_ref[0:CTX_LEN, :] = _conv_centred(xc_ref[...].astype(F32), cw, cb)
    xconv_ref[CTX_LEN:N_KEYS, :] = _conv_centred(xl_ref[...].astype(F32), cw, cb)

    z = -al_ref[...]
    sp = jnp.maximum(z, 0.0) + jnp.log1p(jnp.exp(-jnp.abs(z)))
    ba = ba_ref[...]
    bi = bi_ref[...]

    def gate_chunk(ci, carry):
        r0 = pl.multiple_of(ci * LRU_ROWS, LRU_ROWS)
        xc = xconv_ref[pl.ds(r0, LRU_ROWS), :]
        xb = xc.astype(BF16)
        for g in range(LRU_SLAB // LRU_BLOCK):
            sl = slice(g * LRU_BLOCK, (g + 1) * LRU_BLOCK)
            for d in range(2):
                ra = jnp.dot(xb[:, sl], wa_ref[d, g].astype(BF16), preferred_element_type=F32) + ba[d:d + 1, sl]
                ri = jnp.dot(xb[:, sl], wi_ref[d, g].astype(BF16), preferred_element_type=F32) + bi[d:d + 1, sl]
                log_a = (-LRU_C * jax.nn.sigmoid(ra)) * sp[d:d + 1, sl]
                a = jnp.exp(log_a)
                u = jnp.sqrt(1.0 - a * a) * (jax.nn.sigmoid(ri) * xc[:, sl])
                a_ref[d, pl.ds(r0, LRU_ROWS), sl] = a
                u_ref[d, pl.ds(r0, LRU_ROWS), sl] = u
        return carry

    lax.fori_loop(0, N_KEYS // LRU_ROWS, gate_chunk, 0)

    def scan_pair(base, length):
        def body(i, carry):
            hf, hb = carry
            tf = base + i
            tb = base + length - 1 - i
            hf = a_ref[0, pl.ds(tf, 1), :] * hf + u_ref[0, pl.ds(tf, 1), :]
            hb = a_ref[1, pl.ds(tb, 1), :] * hb + u_ref[1, pl.ds(tb, 1), :]
            hf_ref[pl.ds(tf, 1), :] = hf
            hb_ref[pl.ds(tb, 1), :] = hb
            return hf, hb
        return body

    zero = jnp.zeros((1, LRU_SLAB), F32)
    carry = lax.fori_loop(0, CTX_LEN, scan_pair(0, CTX_LEN), (zero, zero), unroll=8)
    lax.fori_loop(0, SEQ, scan_pair(CTX_LEN, SEQ), carry, unroll=8)

    def out_chunk(ci, carry):
        r0 = pl.multiple_of(ci * LRU_ROWS, LRU_ROWS)
        h = hf_ref[pl.ds(CTX_LEN + r0, LRU_ROWS), :] + hb_ref[pl.ds(CTX_LEN + r0, LRU_ROWS), :]
        yl_ref[pl.ds(r0, LRU_ROWS), :] = (h * _gelu_tanh(gl_ref[pl.ds(r0, LRU_ROWS), :].astype(F32))).astype(BF16)
        return carry

    lax.fori_loop(0, SEQ // LRU_ROWS, out_chunk, 0)
    if need_ctx:
        h = hf_ref[0:CTX_LEN, :] + hb_ref[0:CTX_LEN, :]
        yc_ref[...] = (h * _gelu_tanh(gc_ref[...].astype(F32))).astype(BF16)


def _rglru(p, conv_w, conv_b, w_a, b_a, w_i, b_i, a_logit, need_ctx):
    slab = LRU_SLAB
    n_slab = LRU_WIDTH // slab
    gpb = slab // LRU_BLOCK
    ctx_blk = N_LAT // CTX_LEN
    x_col = COL_RX // slab
    g_col = COL_RG // slab
    in_specs = [
        pl.BlockSpec((SEQ, slab), lambda b, s: (b, x_col + s)),
        pl.BlockSpec((CTX_LEN, slab), lambda b, s: (ctx_blk + b, x_col + s)),
        pl.BlockSpec((SEQ, slab), lambda b, s: (b, g_col + s)),
    ]
    args = [p, p, p]
    if need_ctx:
        in_specs.append(pl.BlockSpec((CTX_LEN, slab), lambda b, s: (ctx_blk + b, g_col + s)))
        args.append(p)
    in_specs += [
        pl.BlockSpec((CONV_WIDTH, slab), lambda b, s: (0, s)),
        pl.BlockSpec((1, slab), lambda b, s: (0, s)),
        pl.BlockSpec((2, gpb, LRU_BLOCK, LRU_BLOCK), lambda b, s: (0, s, 0, 0)),
        pl.BlockSpec((2, slab), lambda b, s: (0, s)),
        pl.BlockSpec((2, gpb, LRU_BLOCK, LRU_BLOCK), lambda b, s: (0, s, 0, 0)),
        pl.BlockSpec((2, slab), lambda b, s: (0, s)),
        pl.BlockSpec((2, slab), lambda b, s: (0, s)),
    ]
    args += [conv_w, conv_b.reshape(1, LRU_WIDTH), w_a, b_a, w_i, b_i, a_logit]
    out_shape = [jax.ShapeDtypeStruct((N_LAT, LRU_WIDTH), BF16)]
    out_specs = [pl.BlockSpec((SEQ, slab), lambda b, s: (b, s))]
    if need_ctx:
        out_shape.append(jax.ShapeDtypeStruct((N_CTX, LRU_WIDTH), BF16))
        out_specs.append(pl.BlockSpec((CTX_LEN, slab), lambda b, s: (b, s)))
    return pl.pallas_call(
        functools.partial(_lru_kernel, need_ctx=need_ctx),
        out_shape=out_shape,
        grid=(BATCH, n_slab),
        in_specs=in_specs,
        out_specs=out_specs,
        scratch_shapes=[
            pltpu.VMEM((N_KEYS, slab), F32),
            pltpu.VMEM((2, N_KEYS, slab), F32),
            pltpu.VMEM((2, N_KEYS, slab), F32),
            pltpu.VMEM((N_KEYS, slab), F32),
            pltpu.VMEM((N_KEYS, slab), F32),
        ],
        compiler_params=_params(2),
        name="rglru",
    )(*args)


def _outproj_kernel(d_ref, g_ref, l_ref, w_ref, x_ref, gate_ref, o_ref, wb_ref):
    @pl.when(pl.program_id(1) == 0)
    def _():
        wb_ref[...] = w_ref[...].astype(BF16)

    acc = jnp.dot(d_ref[...], wb_ref[0:DIFF_WIDTH, :], preferred_element_type=F32)
    acc += jnp.dot(g_ref[...], wb_ref[DIFF_WIDTH:DIFF_WIDTH + GQA_WIDTH, :], preferred_element_type=F32)
    acc += jnp.dot(l_ref[...], wb_ref[DIFF_WIDTH + GQA_WIDTH:, :], preferred_element_type=F32)
    o_ref[...] = x_ref[...] + gate_ref[...] * acc


def _out_projection(diff, gqa, lru, w_out_l, x, mods, layer, rows):
    tm, tn = 512, 512
    return pl.pallas_call(
        _outproj_kernel,
        out_shape=jax.ShapeDtypeStruct((rows, D_MODEL), F32),
        grid=(D_MODEL // tn, rows // tm),
        in_specs=[
            pl.BlockSpec((tm, DIFF_WIDTH), lambda j, i: (i, 0)),
            pl.BlockSpec((tm, GQA_WIDTH), lambda j, i: (i, 0)),
            pl.BlockSpec((tm, LRU_WIDTH), lambda j, i: (i, 0)),
            pl.BlockSpec((D_MODEL, tn), lambda j, i: (0, j)),
            pl.BlockSpec((tm, tn), lambda j, i: (i, j)),
            _mod_spec(layer, 2, tm, tn, 1, 0),
        ],
        out_specs=pl.BlockSpec((tm, tn), lambda j, i: (i, j)),
        scratch_shapes=[pltpu.VMEM((D_MODEL, tn), BF16)],
        compiler_params=_params(2),
        name="out_projection",
    )(diff, gqa, lru, w_out_l, x, mods)


MLP_TM = 512
MLP_TF = 512
MLP_TN = 1024


def _mlp_kernel(x_ref, sh_ref, sc_ref, gate_ref, w1_ref, w2_ref, fin_ref, o_ref, h_ref, *, final):
    f = pl.program_id(1)

    @pl.when(f == 0)
    def _():
        y = _rms(x_ref[...])
        h_ref[...] = (y * (1.0 + sc_ref[...]) + sh_ref[...]).astype(BF16)

    t = jnp.dot(h_ref[...], w1_ref[...], preferred_element_type=F32)
    t = jnp.maximum(t, 0.0)
    tb = (t * t).astype(BF16)

    for c in range(D_MODEL // MLP_TN):
        sl = slice(c * MLP_TN, (c + 1) * MLP_TN)
        part = jnp.dot(tb, w2_ref[:, sl], preferred_element_type=F32)

        @pl.when(f == 0)
        def _():
            o_ref[:, sl] = part

        @pl.when(f != 0)
        def _():
            o_ref[:, sl] += part

    @pl.when(f == pl.num_programs(1) - 1)
    def _():
        y = x_ref[...] + gate_ref[...] * o_ref[...]
        if final:
            y = _rms(y) * fin_ref[...]
        o_ref[...] = y


def _mlp(x, mods, w1b, w2b, final_norm, layer, rows, final):
    tm, tf = MLP_TM, MLP_TF
    return pl.pallas_call(
        functools.partial(_mlp_kernel, final=final),
        out_shape=jax.ShapeDtypeStruct((rows, D_MODEL), F32),
        grid=(rows // tm, D_FF // tf),
        in_specs=[
            pl.BlockSpec((tm, D_MODEL), lambda i, f: (i, 0), pipeline_mode=pl.Buffered(1)),
            _mod_spec(layer, 3, tm, D_MODEL, 0, None),
            _mod_spec(layer, 4, tm, D_MODEL, 0, None),
            _mod_spec(layer, 5, tm, D_MODEL, 0, None),
            pl.BlockSpec((D_MODEL, tf), lambda i, f: (0, f)),
            pl.BlockSpec((tf, D_MODEL), lambda i, f: (f, 0)),
            pl.BlockSpec((1, D_MODEL), lambda i, f: (0, 0)),
        ],
        out_specs=pl.BlockSpec((tm, D_MODEL), lambda i, f: (i, 0)),
        scratch_shapes=[pltpu.VMEM((tm, D_MODEL), BF16)],
        compiler_params=_params(2),
        name="mlp_final" if final else "mlp",
    )(x, mods, mods, mods, w1b, w2b, final_norm.reshape(1, D_MODEL))


def kernel(x, c, ctx, c_ctx, w_mod, b_mod, w_in, diff_lambda, diff_subln, gqa_q_norm, gqa_k_norm,
           lru_conv_w, lru_conv_b, lru_w_a, lru_b_a, lru_w_i, lru_b_i, lru_a_logit,
           w_out, w_ff1, w_ff2, final_norm):
    depth = w_mod.shape[0]
    c8 = jnp.concatenate([c, c_ctx[None], jnp.zeros((8 - BATCH - 1, D_MODEL), F32)], axis=0)
    mods = _modulation(c8, w_mod, b_mod).reshape(depth * 8 * N_MOD, 1, D_MODEL)
    tables = _rope_tables(DIFF_QK_DIM) + _rope_tables(HEAD_DIM)

    xt = jnp.concatenate([x.reshape(N_LAT, D_MODEL), ctx.reshape(N_CTX, D_MODEL)], axis=0)
    for l in range(depth):
        need_ctx = l < depth - 1
        rows = N_TOK if need_ctx else N_LAT
        h = _norm_mod(xt, mods, l)
        p = _in_projection(h, w_in[l])
        qd0, qd1, kd, qg, kg = _qk_prep(p, gqa_q_norm[l], gqa_k_norm[l], tables)
        diff = _diff_attention(qd0, qd1, kd, p, diff_lambda[l], diff_subln[l], l, True)
        gqa = _gqa_attention(qg, kg, p, True)
        lru = _rglru(p, lru_conv_w[l], lru_conv_b[l], lru_w_a[l], lru_b_a[l], lru_w_i[l], lru_b_i[l],
                     lru_a_logit[l], need_ctx)
        if need_ctx:
            diff = jnp.concatenate([diff, _diff_attention(qd0, qd1, kd, p, diff_lambda[l], diff_subln[l], l, False)], 0)
            gqa = jnp.concatenate([gqa, _gqa_attention(qg, kg, p, False)], 0)
            lru = jnp.concatenate(lru, 0)
        else:
            lru = lru[0]
        xt = _out_projection(diff, gqa, lru, w_out[l], xt, mods, l, rows)
        xt = _mlp(xt, mods, w_ff1[l].astype(BF16), w_ff2[l].astype(BF16), final_norm, l, rows, not need_ctx)
    return xt.reshape(BATCH, SEQ, D_MODEL)
```

```python
import functools
import math

import jax
import jax.numpy as jnp
from jax import lax
from jax.experimental import pallas as pl
from jax.experimental.pallas import tpu as pltpu

D_MODEL = 4096
BATCH = 4
SEQ = 2048
DEPTH = 2
GRID_W = 64
CTX_LEN = 256
HEAD_DIM = 128
DIFF_WIDTH = D_MODEL // 4
DIFF_HEADS = DIFF_WIDTH // HEAD_DIM
DIFF_QK_DIM = HEAD_DIM // 2
GQA_WIDTH = D_MODEL // 4
GQA_HEADS = GQA_WIDTH // HEAD_DIM
GQA_KV_HEADS = 2
GQA_GROUP = GQA_HEADS // GQA_KV_HEADS
LRU_WIDTH = D_MODEL // 2
LRU_BLOCK = 128
LRU_BLOCKS = LRU_WIDTH // LRU_BLOCK
CONV_WIDTH = 4
LRU_C = 8.0
D_FF = 4 * D_MODEL
IN_COLS = 3 * DIFF_WIDTH + GQA_WIDTH + 2 * GQA_KV_HEADS * HEAD_DIM + 2 * LRU_WIDTH
ROPE_THETA = 10000.0
EPS = 1e-6
N_MOD = 6

N_LAT = BATCH * SEQ
N_CTX = BATCH * CTX_LEN
N_TOK = N_LAT + N_CTX
N_KEYS = CTX_LEN + SEQ

COL_DQ = 0
COL_DK = COL_DQ + DIFF_WIDTH
COL_DV = COL_DK + DIFF_WIDTH
COL_GQ = COL_DV + DIFF_WIDTH
COL_GK = COL_GQ + GQA_WIDTH
COL_GV = COL_GK + GQA_KV_HEADS * HEAD_DIM
COL_RX = COL_GV + GQA_KV_HEADS * HEAD_DIM
COL_RG = COL_RX + LRU_WIDTH

LANES = 128
SUBLANES = 8
VMEM_LIMIT = 56 * 1024 * 1024

F32 = jnp.float32
BF16 = jnp.bfloat16


def _params(n_axes, vmem=VMEM_LIMIT):
    return pltpu.CompilerParams(dimension_semantics=("arbitrary",) * n_axes, vmem_limit_bytes=vmem)


def _rms(x):
    return x * lax.rsqrt(jnp.mean(x * x, axis=-1, keepdims=True) + EPS)


def _mod_row(tile, tm):
    return jnp.minimum((tile * tm) // SEQ, BATCH)


def _mod_kernel(c_ref, w_ref, b_ref, o_ref):
    c = c_ref[...]
    s = (c * jax.nn.sigmoid(c)).astype(BF16)
    o_ref[...] = jnp.dot(s, w_ref[...].astype(BF16), preferred_element_type=F32) + b_ref[...]


def _modulation(c8, w_mod, b_mod):
    depth = w_mod.shape[0]
    tn = 512
    return pl.pallas_call(
        _mod_kernel,
        out_shape=jax.ShapeDtypeStruct((depth, 8, N_MOD * D_MODEL), F32),
        grid=(depth, N_MOD * D_MODEL // tn),
        in_specs=[
            pl.BlockSpec((8, D_MODEL), lambda l, j: (0, 0)),
            pl.BlockSpec((None, D_MODEL, tn), lambda l, j: (l, 0, j)),
            pl.BlockSpec((None, 1, tn), lambda l, j: (l, 0, j)),
        ],
        out_specs=pl.BlockSpec((None, 8, tn), lambda l, j: (l, 0, j)),
        compiler_params=_params(2),
        name="modulation",
    )(c8, w_mod, b_mod.reshape(depth, 1, N_MOD * D_MODEL))


def _mod_spec(layer, which, tm, tn, row_axis, col_axis):
    def index_map(*ids):
        row = _mod_row(ids[row_axis], tm)
        col = 0 if col_axis is None else ids[col_axis]
        return (layer * 8 * N_MOD + row * N_MOD + which, 0, col)
    return pl.BlockSpec((None, 1, tn), index_map)


def _norm_mod_kernel(x_ref, sh_ref, sc_ref, o_ref):
    y = _rms(x_ref[...])
    o_ref[...] = (y * (1.0 + sc_ref[...]) + sh_ref[...]).astype(BF16)


def _norm_mod(x, mods, layer):
    tm = 256
    return pl.pallas_call(
        _norm_mod_kernel,
        out_shape=jax.ShapeDtypeStruct((N_TOK, D_MODEL), BF16),
        grid=(N_TOK // tm,),
        in_specs=[
            pl.BlockSpec((tm, D_MODEL), lambda i: (i, 0)),
            _mod_spec(layer, 0, tm, D_MODEL, 0, None),
            _mod_spec(layer, 1, tm, D_MODEL, 0, None),
        ],
        out_specs=pl.BlockSpec((tm, D_MODEL), lambda i: (i, 0)),
        compiler_params=_params(1),
        name="norm_mod",
    )(x, mods, mods)


def _inproj_kernel(h_ref, w_ref, o_ref, wb_ref):
    @pl.when(pl.program_id(1) == 0)
    def _():
        wb_ref[...] = w_ref[...].astype(BF16)

    o_ref[...] = jnp.dot(h_ref[...], wb_ref[...], preferred_element_type=F32).astype(o_ref.dtype)


def _in_projection(h, w_in, layer):
    tm, tn = 1024, 512
    return pl.pallas_call(
        _inproj_kernel,
        out_shape=jax.ShapeDtypeStruct((N_TOK, IN_COLS), BF16),
        grid=(IN_COLS // tn, N_TOK // tm),
        in_specs=[
            pl.BlockSpec((tm, D_MODEL), lambda j, i: (i, 0)),
            pl.BlockSpec((None, D_MODEL, tn), lambda j, i: (layer, 0, j)),
        ],
        out_specs=pl.BlockSpec((tm, tn), lambda j, i: (i, j)),
        scratch_shapes=[pltpu.VMEM((D_MODEL, tn), BF16)],
        compiler_params=_params(2),
        name="in_projection",
    )(h, w_in)


def _rope_tables(dim):
    n_freq = dim // 4
    pos = jnp.arange(SEQ)
    row = (pos // GRID_W).astype(F32)
    col = (pos % GRID_W).astype(F32)
    inv = ROPE_THETA ** (-jnp.arange(n_freq, dtype=F32) / n_freq)
    ang_r = row[:, None] * inv
    ang_c = col[:, None] * inv
    cos = jnp.concatenate([jnp.cos(ang_r), jnp.cos(ang_r), jnp.cos(ang_c), jnp.cos(ang_c)], axis=1)
    sin = jnp.concatenate([-jnp.sin(ang_r), jnp.sin(ang_r), -jnp.sin(ang_c), jnp.sin(ang_c)], axis=1)
    reps = LANES // dim
    cos = jnp.tile(cos, (1, reps))
    sin = jnp.tile(sin, (1, reps))
    cos = jnp.concatenate([cos, jnp.ones((CTX_LEN, LANES), F32)], axis=0)
    sin = jnp.concatenate([sin, jnp.zeros((CTX_LEN, LANES), F32)], axis=0)
    return cos, sin


def _rope(x, cos, sin, first, half):
    partner = jnp.where(first, pltpu.roll(x, LANES - half, 1), pltpu.roll(x, half, 1))
    return x * cos + partner * sin


def _qkprep_kernel(dq_ref, dk_ref, gq_ref, gk_ref, cd_ref, sd_ref, cg_ref, sg_ref, qgain_ref, kgain_ref,
                   qd0_ref, qd1_ref, kd_ref, qg_ref, kg_ref):
    cd, sd, cg, sg = cd_ref[...], sd_ref[...], cg_ref[...], sg_ref[...]
    qgain, kgain = qgain_ref[...], kgain_ref[...]
    lane = lax.broadcasted_iota(jnp.int32, cd.shape, 1)
    first_d = (lane & (DIFF_QK_DIM // 2 - 1)) < DIFF_QK_DIM // 4
    first_g = (lane & (HEAD_DIM // 2 - 1)) < HEAD_DIM // 4
    map0 = lane < DIFF_QK_DIM
    d_scale = DIFF_QK_DIM ** -0.5
    g_scale = HEAD_DIM ** -0.5
    for c in range(DIFF_WIDTH // LANES):
        sl = slice(c * LANES, (c + 1) * LANES)
        q = _rope(dq_ref[:, sl].astype(F32), cd, sd, first_d, DIFF_QK_DIM // 4) * d_scale
        qd0_ref[:, sl] = jnp.where(map0, q, 0.0).astype(BF16)
        qd1_ref[:, sl] = jnp.where(map0, 0.0, q).astype(BF16)
        kd_ref[:, sl] = _rope(dk_ref[:, sl].astype(F32), cd, sd, first_d, DIFF_QK_DIM // 4).astype(BF16)
    for c in range(GQA_WIDTH // LANES):
        sl = slice(c * LANES, (c + 1) * LANES)
        qn = _rms(gq_ref[:, sl].astype(F32)) * qgain
        qg_ref[:, sl] = (_rope(qn, cg, sg, first_g, HEAD_DIM // 4) * g_scale).astype(BF16)
    for c in range(GQA_KV_HEADS):
        sl = slice(c * LANES, (c + 1) * LANES)
        kn = _rms(gk_ref[:, sl].astype(F32)) * kgain
        kg_ref[:, sl] = _rope(kn, cg, sg, first_g, HEAD_DIM // 4).astype(BF16)


def _qk_prep(p, q_gain, k_gain, tables, layer):
    tm = 256
    cos_d, sin_d, cos_g, sin_g = tables
    lat_tiles = N_LAT // tm

    def pos_block(i):
        return jnp.where(i < lat_tiles, i % (SEQ // tm), SEQ // tm)

    kv_w = GQA_KV_HEADS * HEAD_DIM
    tab_spec = pl.BlockSpec((tm, LANES), lambda i: (pos_block(i), 0))
    gain_spec = pl.BlockSpec((None, 1, HEAD_DIM), lambda i: (layer, 0, 0))
    wide = lambda col: pl.BlockSpec((tm, DIFF_WIDTH), lambda i: (i, col // DIFF_WIDTH))
    return pl.pallas_call(
        _qkprep_kernel,
        out_shape=[
            jax.ShapeDtypeStruct((N_TOK, DIFF_WIDTH), BF16),
            jax.ShapeDtypeStruct((N_TOK, DIFF_WIDTH), BF16),
            jax.ShapeDtypeStruct((N_TOK, DIFF_WIDTH), BF16),
            jax.ShapeDtypeStruct((N_TOK, GQA_WIDTH), BF16),
            jax.ShapeDtypeStruct((N_TOK, kv_w), BF16),
        ],
        grid=(N_TOK // tm,),
        in_specs=[
            wide(COL_DQ), wide(COL_DK), wide(COL_GQ),
            pl.BlockSpec((tm, kv_w), lambda i: (i, COL_GK // kv_w)),
            tab_spec, tab_spec, tab_spec, tab_spec, gain_spec, gain_spec,
        ],
        out_specs=[
            pl.BlockSpec((tm, DIFF_WIDTH), lambda i: (i, 0)),
            pl.BlockSpec((tm, DIFF_WIDTH), lambda i: (i, 0)),
            pl.BlockSpec((tm, DIFF_WIDTH), lambda i: (i, 0)),
            pl.BlockSpec((tm, GQA_WIDTH), lambda i: (i, 0)),
            pl.BlockSpec((tm, kv_w), lambda i: (i, 0)),
        ],
        compiler_params=_params(1),
        name="qk_prep",
    )(p, p, p, p, cos_d, sin_d, cos_g, sin_g, q_gain.reshape(-1, 1, HEAD_DIM), k_gain.reshape(-1, 1, HEAD_DIM))


def _scores(qq, k_refs):
    return [lax.dot_general(qq, k[...], (((1,), (1,)), ((), ())), preferred_element_type=F32) for k in k_refs]


def _softmax_pv(ss, v_refs):
    m = ss[0].max(axis=-1, keepdims=True)
    for s in ss[1:]:
        m = jnp.maximum(m, s.max(axis=-1, keepdims=True))
    acc = None
    den = None
    for s, v in zip(ss, v_refs):
        e = jnp.exp(s - m)
        d = e.sum(axis=-1, keepdims=True)
        o = jnp.dot(e.astype(BF16), v[...], preferred_element_type=F32)
        acc = o if acc is None else acc + o
        den = d if den is None else den + d
    return acc, den


def _diff_attn_kernel(lam_ref, gain_ref, q0_ref, q1_ref, *refs, lam_init, n_blocks):
    k_refs = refs[0:2 * n_blocks:2]
    v_refs = refs[1:2 * n_blocks:2]
    o_ref = refs[2 * n_blocks]
    tq = q0_ref.shape[0]
    lv = lam_ref[...]
    lam = (jnp.exp(jnp.sum(lv[0:1] * lv[1:2], axis=-1, keepdims=True))
           - jnp.exp(jnp.sum(lv[2:3] * lv[3:4], axis=-1, keepdims=True)) + lam_init)
    qq = jnp.concatenate([q0_ref[...], q1_ref[...]], axis=0)
    acc, den = _softmax_pv(_scores(qq, k_refs), v_refs)
    o2 = acc / den
    o = o2[:tq] - lam * o2[tq:]
    o_ref[...] = (_rms(o) * gain_ref[...] * (1.0 - lam_init)).astype(o_ref.dtype)


def _diff_attention(qd0, qd1, kd, p, lam_vecs, subln, layer_idx, latent):
    lam_init = 0.8 - 0.6 * math.exp(-0.3 * layer_idx)
    v_col = COL_DV // HEAD_DIM
    ctx_blk = N_LAT // CTX_LEN
    if latent:
        tq = 256
        nq = SEQ // tq
        rows = N_LAT
        q_spec = pl.BlockSpec((tq, HEAD_DIM), lambda b, h, i: (b * nq + i, h))
        kv_specs = [
            pl.BlockSpec((CTX_LEN, HEAD_DIM), lambda b, h, i: (ctx_blk + b, h)),
            pl.BlockSpec((CTX_LEN, HEAD_DIM), lambda b, h, i: (ctx_blk + b, v_col + h)),
            pl.BlockSpec((SEQ, HEAD_DIM), lambda b, h, i: (b, h)),
            pl.BlockSpec((SEQ, HEAD_DIM), lambda b, h, i: (b, v_col + h)),
        ]
        kv_args = (kd, p, kd, p)
        out_spec = pl.BlockSpec((tq, HEAD_DIM), lambda b, h, i: (b * nq + i, h))
    else:
        tq = CTX_LEN
        nq = 1
        rows = N_CTX
        q_spec = pl.BlockSpec((tq, HEAD_DIM), lambda b, h, i: (ctx_blk + b, h))
        kv_specs = [
            pl.BlockSpec((CTX_LEN, HEAD_DIM), lambda b, h, i: (ctx_blk + b, h)),
            pl.BlockSpec((CTX_LEN, HEAD_DIM), lambda b, h, i: (ctx_blk + b, v_col + h)),
        ]
        kv_args = (kd, p)
        out_spec = pl.BlockSpec((tq, HEAD_DIM), lambda b, h, i: (b, h))
    return pl.pallas_call(
        functools.partial(_diff_attn_kernel, lam_init=lam_init, n_blocks=len(kv_args) // 2),
        out_shape=jax.ShapeDtypeStruct((rows, DIFF_WIDTH), BF16),
        grid=(BATCH, DIFF_HEADS, nq),
        in_specs=[
            pl.BlockSpec((None, 4, DIFF_QK_DIM), lambda b, h, i: (layer_idx, 0, 0)),
            pl.BlockSpec((None, 1, HEAD_DIM), lambda b, h, i: (layer_idx, 0, 0)),
            q_spec, q_spec, *kv_specs,
        ],
        out_specs=out_spec,
        compiler_params=_params(3),
        name="diff_attention_lat" if latent else "diff_attention_ctx",
    )(lam_vecs, subln.reshape(-1, 1, HEAD_DIM), qd0, qd1, *kv_args)


def _gqa_attn_kernel(q_ref, *refs, n_blocks):
    k_refs = refs[0:2 * n_blocks:2]
    v_refs = refs[1:2 * n_blocks:2]
    o_ref = refs[2 * n_blocks]
    tq = q_ref.shape[0]
    qq = jnp.concatenate([q_ref[:, g * HEAD_DIM:(g + 1) * HEAD_DIM] for g in range(GQA_GROUP)], axis=0)
    acc, den = _softmax_pv(_scores(qq, k_refs), v_refs)
    o = acc / den
    for g in range(GQA_GROUP):
        o_ref[:, g * HEAD_DIM:(g + 1) * HEAD_DIM] = o[g * tq:(g + 1) * tq].astype(o_ref.dtype)


def _gqa_attention(qg, kg, p, latent):
    v_col = COL_GV // HEAD_DIM
    ctx_blk = N_LAT // CTX_LEN
    gw = GQA_GROUP * HEAD_DIM
    if latent:
        tq = 128
        nq = SEQ // tq
        rows = N_LAT
        q_spec = pl.BlockSpec((tq, gw), lambda b, g, i: (b * nq + i, g))
        kv_specs = [
            pl.BlockSpec((CTX_LEN, HEAD_DIM), lambda b, g, i: (ctx_blk + b, g)),
            pl.BlockSpec((CTX_LEN, HEAD_DIM), lambda b, g, i: (ctx_blk + b, v_col + g)),
            pl.BlockSpec((SEQ, HEAD_DIM), lambda b, g, i: (b, g)),
            pl.BlockSpec((SEQ, HEAD_DIM), lambda b, g, i: (b, v_col + g)),
        ]
        kv_args = (kg, p, kg, p)
        out_spec = pl.BlockSpec((tq, gw), lambda b, g, i: (b * nq + i, g))
    else:
        tq = CTX_LEN
        nq = 1
        rows = N_CTX
        q_spec = pl.BlockSpec((tq, gw), lambda b, g, i: (ctx_blk + b, g))
        kv_specs = [
            pl.BlockSpec((CTX_LEN, HEAD_DIM), lambda b, g, i: (ctx_blk + b, g)),
            pl.BlockSpec((CTX_LEN, HEAD_DIM), lambda b, g, i: (ctx_blk + b, v_col + g)),
        ]
        kv_args = (kg, p)
        out_spec = pl.BlockSpec((tq, gw), lambda b, g, i: (b, g))
    return pl.pallas_call(
        functools.partial(_gqa_attn_kernel, n_blocks=len(kv_args) // 2),
        out_shape=jax.ShapeDtypeStruct((rows, GQA_WIDTH), BF16),
        grid=(BATCH, GQA_KV_HEADS, nq),
        in_specs=[q_spec, *kv_specs],
        out_specs=out_spec,
        compiler_params=_params(3),
        name="gqa_attention_lat" if latent else "gqa_attention_ctx",
    )(qg, *kv_args)


LRU_SLAB = 256
LRU_ROWS = 256


def _conv_centred(x, w, bias):
    n = x.shape[0]
    row = lax.broadcasted_iota(jnp.int32, x.shape, 0)
    xm1 = jnp.where(row >= 1, pltpu.roll(x, 1, 0), 0.0)
    xp1 = jnp.where(row < n - 1, pltpu.roll(x, n - 1, 0), 0.0)
    xp2 = jnp.where(row < n - 2, pltpu.roll(x, n - 2, 0), 0.0)
    return xm1 * w[0:1] + x * w[1:2] + xp1 * w[2:3] + xp2 * w[3:4] + bias


def _gelu_tanh(g):
    return 0.5 * g * (1.0 + jnp.tanh(0.7978845608028654 * (g + 0.044715 * (g * g * g))))


def _lru_kernel(*refs, need_ctx):
    if need_ctx:
        (xl_ref, xc_ref, gl_ref, gc_ref, cw_ref, cb_ref, wa_ref, ba_ref, wi_ref, bi_ref, al_ref,
         yl_ref, yc_ref, xconv_ref, a_ref, u_ref, hf_ref, hb_ref) = refs
    else:
        (xl_ref, xc_ref, gl_ref, cw_ref, cb_ref, wa_ref, ba_ref, wi_ref, bi_ref, al_ref,
         yl_ref, xconv_ref, a_ref, u_ref, hf_ref, hb_ref) = refs
    cw = cw_ref[...]
    cb = cb_ref[...]
    xconv_ref[0:CTX_LEN, :] = _conv_centred(xc_ref[...].astype(F32), cw, cb)
    xconv_ref[CTX_LEN:N_KEYS, :] = _conv_centred(xl_ref[...].astype(F32), cw, cb)

    z = -al_ref[...]
    sp = jnp.maximum(z, 0.0) + jnp.log1p(jnp.exp(-jnp.abs(z)))
    ba = ba_ref[...]
    bi = bi_ref[...]

    def gate_chunk(ci, carry):
        r0 = pl.multiple_of(ci * LRU_ROWS, LRU_ROWS)
        xc = xconv_ref[pl.ds(r0, LRU_ROWS), :]
        xb = xc.astype(BF16)
        for g in range(LRU_SLAB // LRU_BLOCK):
            sl = slice(g * LRU_BLOCK, (g + 1) * LRU_BLOCK)
            for d in range(2):
                ra = jnp.dot(xb[:, sl], wa_ref[d, g].astype(BF16), preferred_element_type=F32) + ba[d:d + 1, sl]
                ri = jnp.dot(xb[:, sl], wi_ref[d, g].astype(BF16), preferred_element_type=F32) + bi[d:d + 1, sl]
                log_a = (-LRU_C * jax.nn.sigmoid(ra)) * sp[d:d + 1, sl]
                a = jnp.exp(log_a)
                u = jnp.sqrt(1.0 - a * a) * (jax.nn.sigmoid(ri) * xc[:, sl])
                a_ref[d, pl.ds(r0, LRU_ROWS), sl] = a
                u_ref[d, pl.ds(r0, LRU_ROWS), sl] = u
        return carry

    lax.fori_loop(0, N_KEYS // LRU_ROWS, gate_chunk, 0)

    row8 = lax.broadcasted_iota(jnp.int32, (SUBLANES, LRU_SLAB), 0)

    def block_scan(a, u, h_in, reverse):
        for k in (1, 2, 4):
            if reverse:
                keep = row8 < SUBLANES - k
                shift = SUBLANES - k
            else:
                keep = row8 >= k
                shift = k
            a_s = jnp.where(keep, pltpu.roll(a, shift, 0), 1.0)
            u_s = jnp.where(keep, pltpu.roll(u, shift, 0), 0.0)
            u = a * u_s + u
            a = a * a_s
        h = a * h_in + u
        last = h[0:1, :] if reverse else h[SUBLANES - 1:SUBLANES, :]
        return h, jnp.broadcast_to(last, h.shape)

    def scan_pair(base, n_blocks):
        def body(i, carry):
            hf, hb = carry
            rf = pl.multiple_of(base + i * SUBLANES, SUBLANES)
            rb = pl.multiple_of(base + (n_blocks - 1 - i) * SUBLANES, SUBLANES)
            h, hf = block_scan(a_ref[0, pl.ds(rf, SUBLANES), :], u_ref[0, pl.ds(rf, SUBLANES), :], hf, False)
            hf_ref[pl.ds(rf, SUBLANES), :] = h
            h, hb = block_scan(a_ref[1, pl.ds(rb, SUBLANES), :], u_ref[1, pl.ds(rb, SUBLANES), :], hb, True)
            hb_ref[pl.ds(rb, SUBLANES), :] = h
            return hf, hb
        return body

    zero = jnp.zeros((SUBLANES, LRU_SLAB), F32)
    carry = lax.fori_loop(0, CTX_LEN // SUBLANES, scan_pair(0, CTX_LEN // SUBLANES), (zero, zero), unroll=2)
    lax.fori_loop(0, SEQ // SUBLANES, scan_pair(CTX_LEN, SEQ // SUBLANES), carry, unroll=2)

    def out_chunk(ci, carry):
        r0 = pl.multiple_of(ci * LRU_ROWS, LRU_ROWS)
        h = hf_ref[pl.ds(CTX_LEN + r0, LRU_ROWS), :] + hb_ref[pl.ds(CTX_LEN + r0, LRU_ROWS), :]
        yl_ref[pl.ds(r0, LRU_ROWS), :] = (h * _gelu_tanh(gl_ref[pl.ds(r0, LRU_ROWS), :].astype(F32))).astype(BF16)
        return carry

    lax.fori_loop(0, SEQ // LRU_ROWS, out_chunk, 0)
    if need_ctx:
        h = hf_ref[0:CTX_LEN, :] + hb_ref[0:CTX_LEN, :]
        yc_ref[...] = (h * _gelu_tanh(gc_ref[...].astype(F32))).astype(BF16)


def _rglru(p, conv_w, conv_b, w_a, b_a, w_i, b_i, a_logit, layer, need_ctx):
    slab = LRU_SLAB
    n_slab = LRU_WIDTH // slab
    gpb = slab // LRU_BLOCK
    ctx_blk = N_LAT // CTX_LEN
    x_col = COL_RX // slab
    g_col = COL_RG // slab
    in_specs = [
        pl.BlockSpec((SEQ, slab), lambda b, s: (b, x_col + s)),
        pl.BlockSpec((CTX_LEN, slab), lambda b, s: (ctx_blk + b, x_col + s)),
        pl.BlockSpec((SEQ, slab), lambda b, s: (b, g_col + s)),
    ]
    args = [p, p, p]
    if need_ctx:
        in_specs.append(pl.BlockSpec((CTX_LEN, slab), lambda b, s: (ctx_blk + b, g_col + s)))
        args.append(p)
    in_specs += [
        pl.BlockSpec((None, CONV_WIDTH, slab), lambda b, s: (layer, 0, s)),
        pl.BlockSpec((None, 1, slab), lambda b, s: (layer, 0, s)),
        pl.BlockSpec((None, 2, gpb, LRU_BLOCK, LRU_BLOCK), lambda b, s: (layer, 0, s, 0, 0)),
        pl.BlockSpec((None, 2, slab), lambda b, s: (layer, 0, s)),
        pl.BlockSpec((None, 2, gpb, LRU_BLOCK, LRU_BLOCK), lambda b, s: (layer, 0, s, 0, 0)),
        pl.BlockSpec((None, 2, slab), lambda b, s: (layer, 0, s)),
        pl.BlockSpec((None, 2, slab), lambda b, s: (layer, 0, s)),
    ]
    args += [conv_w, conv_b.reshape(-1, 1, LRU_WIDTH), w_a, b_a, w_i, b_i, a_logit]
    out_shape = [jax.ShapeDtypeStruct((N_LAT, LRU_WIDTH), BF16)]
    out_specs = [pl.BlockSpec((SEQ, slab), lambda b, s: (b, s))]
    if need_ctx:
        out_shape.append(jax.ShapeDtypeStruct((N_CTX, LRU_WIDTH), BF16))
        out_specs.append(pl.BlockSpec((CTX_LEN, slab), lambda b, s: (b, s)))
    return pl.pallas_call(
        functools.partial(_lru_kernel, need_ctx=need_ctx),
        out_shape=out_shape,
        grid=(BATCH, n_slab),
        in_specs=in_specs,
        out_specs=out_specs,
        scratch_shapes=[
            pltpu.VMEM((N_KEYS, slab), F32),
            pltpu.VMEM((2, N_KEYS, slab), F32),
            pltpu.VMEM((2, N_KEYS, slab), F32),
            pltpu.VMEM((N_KEYS, slab), F32),
            pltpu.VMEM((N_KEYS, slab), F32),
        ],
        compiler_params=_params(2),
        name="rglru",
    )(*args)


def _outproj_kernel(d_ref, g_ref, l_ref, w_ref, x_ref, gate_ref, o_ref, wb_ref):
    @pl.when(pl.program_id(1) == 0)
    def _():
        wb_ref[...] = w_ref[...].astype(BF16)

    acc = jnp.dot(d_ref[...], wb_ref[0:DIFF_WIDTH, :], preferred_element_type=F32)
    acc += jnp.dot(g_ref[...], wb_ref[DIFF_WIDTH:DIFF_WIDTH + GQA_WIDTH, :], preferred_element_type=F32)
    acc += jnp.dot(l_ref[...], wb_ref[DIFF_WIDTH + GQA_WIDTH:, :], preferred_element_type=F32)
    o_ref[...] = x_ref[...] + gate_ref[...] * acc


def _out_projection(diff, gqa, lru, w_out, x, mods, layer, rows):
    tm, tn = 512, 512
    return pl.pallas_call(
        _outproj_kernel,
        out_shape=jax.ShapeDtypeStruct((rows, D_MODEL), F32),
        grid=(D_MODEL // tn, rows // tm),
        in_specs=[
            pl.BlockSpec((tm, DIFF_WIDTH), lambda j, i: (i, 0)),
            pl.BlockSpec((tm, GQA_WIDTH), lambda j, i: (i, 0)),
            pl.BlockSpec((tm, LRU_WIDTH), lambda j, i: (i, 0)),
            pl.BlockSpec((None, D_MODEL, tn), lambda j, i: (layer, 0, j)),
            pl.BlockSpec((tm, tn), lambda j, i: (i, j)),
            _mod_spec(layer, 2, tm, tn, 1, 0),
        ],
        out_specs=pl.BlockSpec((tm, tn), lambda j, i: (i, j)),
        scratch_shapes=[pltpu.VMEM((D_MODEL, tn), BF16)],
        compiler_params=_params(2),
        name="out_projection",
    )(diff, gqa, lru, w_out, x, mods)


MLP_TM = 512
MLP_TF = 512
MLP_TN = 1024


def _mlp_kernel(x_ref, sh_ref, sc_ref, gate_ref, w1_ref, w2_ref, fin_ref, o_ref, h_ref, *, final):
    f = pl.program_id(1)

    @pl.when(f == 0)
    def _():
        y = _rms(x_ref[...])
        h_ref[...] = (y * (1.0 + sc_ref[...]) + sh_ref[...]).astype(BF16)
        o_ref[...] = jnp.zeros_like(o_ref)

    t = jnp.dot(h_ref[...], w1_ref[...], preferred_element_type=F32)
    t = jnp.maximum(t, 0.0)
    tb = (t * t).astype(BF16)

    for c in range(D_MODEL // MLP_TN):
        sl = slice(c * MLP_TN, (c + 1) * MLP_TN)
        o_ref[:, sl] += jnp.dot(tb, w2_ref[:, sl], preferred_element_type=F32)

    @pl.when(f == pl.num_programs(1) - 1)
    def _():
        y = x_ref[...] + gate_ref[...] * o_ref[...]
        if final:
            y = _rms(y) * fin_ref[...]
        o_ref[...] = y


def _mlp(x, mods, w1b, w2b, final_norm, layer, rows, final):
    tm, tf = MLP_TM, MLP_TF
    return pl.pallas_call(
        functools.partial(_mlp_kernel, final=final),
        out_shape=jax.ShapeDtypeStruct((rows, D_MODEL), F32),
        grid=(rows // tm, D_FF // tf),
        in_specs=[
            pl.BlockSpec((tm, D_MODEL), lambda i, f: (i, 0), pipeline_mode=pl.Buffered(1)),
            _mod_spec(layer, 3, tm, D_MODEL, 0, None),
            _mod_spec(layer, 4, tm, D_MODEL, 0, None),
            _mod_spec(layer, 5, tm, D_MODEL, 0, None),
            pl.BlockSpec((None, D_MODEL, tf), lambda i, f: (layer, 0, f)),
            pl.BlockSpec((None, tf, D_MODEL), lambda i, f: (layer, f, 0)),
            pl.BlockSpec((1, D_MODEL), lambda i, f: (0, 0)),
        ],
        out_specs=pl.BlockSpec((tm, D_MODEL), lambda i, f: (i, 0)),
        scratch_shapes=[pltpu.VMEM((tm, D_MODEL), BF16)],
        compiler_params=_params(2),
        name="mlp_final" if final else "mlp",
    )(x, mods, mods, mods, w1b, w2b, final_norm.reshape(1, D_MODEL))


def kernel(x, c, ctx, c_ctx, w_mod, b_mod, w_in, diff_lambda, diff_subln, gqa_q_norm, gqa_k_norm,
           lru_conv_w, lru_conv_b, lru_w_a, lru_b_a, lru_w_i, lru_b_i, lru_a_logit,
           w_out, w_ff1, w_ff2, final_norm):
    depth = w_mod.shape[0]
    c8 = jnp.concatenate([c, c_ctx[None], jnp.zeros((8 - BATCH - 1, D_MODEL), F32)], axis=0)
    mods = _modulation(c8, w_mod, b_mod).reshape(depth * 8 * N_MOD, 1, D_MODEL)
    tables = _rope_tables(DIFF_QK_DIM) + _rope_tables(HEAD_DIM)

    w1b = w_ff1.astype(BF16)
    w2b = w_ff2.astype(BF16)
    xt = jnp.concatenate([x.reshape(N_LAT, D_MODEL), ctx.reshape(N_CTX, D_MODEL)], axis=0)
    for l in range(depth):
        need_ctx = l < depth - 1
        rows = N_TOK if need_ctx else N_LAT
        h = _norm_mod(xt, mods, l)
        p = _in_projection(h, w_in, l)
        qd0, qd1, kd, qg, kg = _qk_prep(p, gqa_q_norm, gqa_k_norm, tables, l)
        diff = _diff_attention(qd0, qd1, kd, p, diff_lambda, diff_subln, l, True)
        gqa = _gqa_attention(qg, kg, p, True)
        lru = _rglru(p, lru_conv_w, lru_conv_b, lru_w_a, lru_b_a, lru_w_i, lru_b_i, lru_a_logit, l, need_ctx)
        if need_ctx:
            diff = jnp.concatenate([diff, _diff_attention(qd0, qd1, kd, p, diff_lambda, diff_subln, l, False)], 0)
            gqa = jnp.concatenate([gqa, _gqa_attention(qg, kg, p, False)], 0)
            lru = jnp.concatenate(lru, 0)
        else:
            lru = lru[0]
        xt = _out_projection(diff, gqa, lru, w_out, xt, mods, l, rows)
        xt = _mlp(xt, mods, w1b, w2b, final_norm, l, rows, not need_ctx)
    return xt.reshape(BATCH, SEQ, D_MODEL)
```

```python
import functools
import math

import jax
import jax.numpy as jnp
from jax import lax
from jax.experimental import pallas as pl
from jax.experimental.pallas import tpu as pltpu

D_MODEL = 4096
BATCH = 4
SEQ = 2048
DEPTH = 2
GRID_W = 64
CTX_LEN = 256
HEAD_DIM = 128
DIFF_WIDTH = D_MODEL // 4
DIFF_HEADS = DIFF_WIDTH // HEAD_DIM
DIFF_QK_DIM = HEAD_DIM // 2
GQA_WIDTH = D_MODEL // 4
GQA_HEADS = GQA_WIDTH // HEAD_DIM
GQA_KV_HEADS = 2
GQA_GROUP = GQA_HEADS // GQA_KV_HEADS
LRU_WIDTH = D_MODEL // 2
LRU_BLOCK = 128
LRU_BLOCKS = LRU_WIDTH // LRU_BLOCK
CONV_WIDTH = 4
LRU_C = 8.0
D_FF = 4 * D_MODEL
IN_COLS = 3 * DIFF_WIDTH + GQA_WIDTH + 2 * GQA_KV_HEADS * HEAD_DIM + 2 * LRU_WIDTH
ROPE_THETA = 10000.0
EPS = 1e-6
N_MOD = 6

N_LAT = BATCH * SEQ
N_CTX = BATCH * CTX_LEN
N_TOK = N_LAT + N_CTX
N_KEYS = CTX_LEN + SEQ

COL_DQ = 0
COL_DK = COL_DQ + DIFF_WIDTH
COL_DV = COL_DK + DIFF_WIDTH
COL_GQ = COL_DV + DIFF_WIDTH
COL_GK = COL_GQ + GQA_WIDTH
COL_GV = COL_GK + GQA_KV_HEADS * HEAD_DIM
COL_RX = COL_GV + GQA_KV_HEADS * HEAD_DIM
COL_RG = COL_RX + LRU_WIDTH

LANES = 128
SUBLANES = 8
VMEM_LIMIT = 56 * 1024 * 1024

F32 = jnp.float32
BF16 = jnp.bfloat16


def _params(n_axes, vmem=VMEM_LIMIT):
    return pltpu.CompilerParams(dimension_semantics=("arbitrary",) * n_axes, vmem_limit_bytes=vmem)


def _rms(x):
    return x * lax.rsqrt(jnp.mean(x * x, axis=-1, keepdims=True) + EPS)


def _mod_row(tile, tm):
    return jnp.minimum((tile * tm) // SEQ, BATCH)


def _mod_kernel(c_ref, w_ref, b_ref, o_ref):
    c = c_ref[...]
    s = (c * jax.nn.sigmoid(c)).astype(BF16)
    o_ref[...] = jnp.dot(s, w_ref[...].astype(BF16), preferred_element_type=F32) + b_ref[...]


def _modulation(c8, w_mod, b_mod):
    depth = w_mod.shape[0]
    tn = 512
    return pl.pallas_call(
        _mod_kernel,
        out_shape=jax.ShapeDtypeStruct((depth, 8, N_MOD * D_MODEL), F32),
        grid=(depth, N_MOD * D_MODEL // tn),
        in_specs=[
            pl.BlockSpec((8, D_MODEL), lambda l, j: (0, 0)),
            pl.BlockSpec((None, D_MODEL, tn), lambda l, j: (l, 0, j)),
            pl.BlockSpec((None, 1, tn), lambda l, j: (l, 0, j)),
        ],
        out_specs=pl.BlockSpec((None, 8, tn), lambda l, j: (l, 0, j)),
        compiler_params=_params(2),
        name="modulation",
    )(c8, w_mod, b_mod.reshape(depth, 1, N_MOD * D_MODEL))


def _mod_spec(layer, which, tm, tn, row_axis, col_axis):
    def index_map(*ids):
        row = _mod_row(ids[row_axis], tm)
        col = 0 if col_axis is None else ids[col_axis]
        return (layer * 8 * N_MOD + row * N_MOD + which, 0, col)
    return pl.BlockSpec((None, 1, tn), index_map)


def _tile_source(i, n_first, refs):
    if len(refs) == 1:
        return refs[0][...]
    return jnp.where(i < n_first, refs[0][...], refs[1][...])


def _split_specs(block, n_first, tile_axis, col_axis, n_sources):
    def col(ids):
        return 0 if col_axis is None else ids[col_axis]
    if n_sources == 1:
        return [pl.BlockSpec(block, lambda *ids: (ids[tile_axis], col(ids)))]
    return [
        pl.BlockSpec(block, lambda *ids: (jnp.minimum(ids[tile_axis], n_first - 1), col(ids))),
        pl.BlockSpec(block, lambda *ids: (jnp.maximum(ids[tile_axis] - n_first, 0), col(ids))),
    ]


def _norm_mod_kernel(*refs, n_first):
    *x_refs, sh_ref, sc_ref, o_ref = refs
    y = _rms(_tile_source(pl.program_id(0), n_first, x_refs))
    o_ref[...] = (y * (1.0 + sc_ref[...]) + sh_ref[...]).astype(BF16)


def _norm_mod(xs, mods, layer):
    tm = 256
    n_first = N_LAT // tm
    return pl.pallas_call(
        functools.partial(_norm_mod_kernel, n_first=n_first),
        out_shape=jax.ShapeDtypeStruct((N_TOK, D_MODEL), BF16),
        grid=(N_TOK // tm,),
        in_specs=[
            *_split_specs((tm, D_MODEL), n_first, 0, None, len(xs)),
            _mod_spec(layer, 0, tm, D_MODEL, 0, None),
            _mod_spec(layer, 1, tm, D_MODEL, 0, None),
        ],
        out_specs=pl.BlockSpec((tm, D_MODEL), lambda i: (i, 0)),
        compiler_params=_params(1),
        name="norm_mod",
    )(*xs, mods, mods)


def _inproj_kernel(h_ref, w_ref, o_ref, wb_ref):
    @pl.when(pl.program_id(1) == 0)
    def _():
        wb_ref[...] = w_ref[...].astype(BF16)

    o_ref[...] = jnp.dot(h_ref[...], wb_ref[...], preferred_element_type=F32).astype(o_ref.dtype)


def _in_projection(h, w_in, layer):
    tm, tn = 1024, 512
    return pl.pallas_call(
        _inproj_kernel,
        out_shape=jax.ShapeDtypeStruct((N_TOK, IN_COLS), BF16),
        grid=(IN_COLS // tn, N_TOK // tm),
        in_specs=[
            pl.BlockSpec((tm, D_MODEL), lambda j, i: (i, 0)),
            pl.BlockSpec((None, D_MODEL, tn), lambda j, i: (layer, 0, j)),
        ],
        out_specs=pl.BlockSpec((tm, tn), lambda j, i: (i, j)),
        scratch_shapes=[pltpu.VMEM((D_MODEL, tn), BF16)],
        compiler_params=_params(2),
        name="in_projection",
    )(h, w_in)


def _rope_tables(dim):
    n_freq = dim // 4
    pos = jnp.arange(SEQ)
    row = (pos // GRID_W).astype(F32)
    col = (pos % GRID_W).astype(F32)
    inv = ROPE_THETA ** (-jnp.arange(n_freq, dtype=F32) / n_freq)
    ang_r = row[:, None] * inv
    ang_c = col[:, None] * inv
    cos = jnp.concatenate([jnp.cos(ang_r), jnp.cos(ang_r), jnp.cos(ang_c), jnp.cos(ang_c)], axis=1)
    sin = jnp.concatenate([-jnp.sin(ang_r), jnp.sin(ang_r), -jnp.sin(ang_c), jnp.sin(ang_c)], axis=1)
    reps = LANES // dim
    cos = jnp.tile(cos, (1, reps))
    sin = jnp.tile(sin, (1, reps))
    cos = jnp.concatenate([cos, jnp.ones((CTX_LEN, LANES), F32)], axis=0)
    sin = jnp.concatenate([sin, jnp.zeros((CTX_LEN, LANES), F32)], axis=0)
    return cos, sin


def _rope(x, cos, sin, first, half):
    partner = jnp.where(first, pltpu.roll(x, LANES - half, 1), pltpu.roll(x, half, 1))
    return x * cos + partner * sin


def _qkprep_kernel(dq_ref, dk_ref, gq_ref, gk_ref, cd_ref, sd_ref, cg_ref, sg_ref, qgain_ref, kgain_ref,
                   qd0_ref, qd1_ref, kd_ref, qg_ref, kg_ref):
    cd, sd, cg, sg = cd_ref[...], sd_ref[...], cg_ref[...], sg_ref[...]
    qgain, kgain = qgain_ref[...], kgain_ref[...]
    lane = lax.broadcasted_iota(jnp.int32, cd.shape, 1)
    first_d = (lane & (DIFF_QK_DIM // 2 - 1)) < DIFF_QK_DIM // 4
    first_g = (lane & (HEAD_DIM // 2 - 1)) < HEAD_DIM // 4
    map0 = lane < DIFF_QK_DIM
    d_scale = DIFF_QK_DIM ** -0.5 * LOG2E
    g_scale = HEAD_DIM ** -0.5 * LOG2E
    for c in range(DIFF_WIDTH // LANES):
        sl = slice(c * LANES, (c + 1) * LANES)
        q = _rope(dq_ref[:, sl].astype(F32), cd, sd, first_d, DIFF_QK_DIM // 4) * d_scale
        qd0_ref[:, sl] = jnp.where(map0, q, 0.0).astype(BF16)
        qd1_ref[:, sl] = jnp.where(map0, 0.0, q).astype(BF16)
        kd_ref[:, sl] = _rope(dk_ref[:, sl].astype(F32), cd, sd, first_d, DIFF_QK_DIM // 4).astype(BF16)
    for c in range(GQA_WIDTH // LANES):
        sl = slice(c * LANES, (c + 1) * LANES)
        qn = _rms(gq_ref[:, sl].astype(F32)) * qgain
        qg_ref[:, sl] = (_rope(qn, cg, sg, first_g, HEAD_DIM // 4) * g_scale).astype(BF16)
    for c in range(GQA_KV_HEADS):
        sl = slice(c * LANES, (c + 1) * LANES)
        kn = _rms(gk_ref[:, sl].astype(F32)) * kgain
        kg_ref[:, sl] = _rope(kn, cg, sg, first_g, HEAD_DIM // 4).astype(BF16)


def _qk_prep(p, q_gain, k_gain, tables, layer):
    tm = 256
    cos_d, sin_d, cos_g, sin_g = tables
    lat_tiles = N_LAT // tm

    def pos_block(i):
        return jnp.where(i < lat_tiles, i % (SEQ // tm), SEQ // tm)

    kv_w = GQA_KV_HEADS * HEAD_DIM
    tab_spec = pl.BlockSpec((tm, LANES), lambda i: (pos_block(i), 0))
    gain_spec = pl.BlockSpec((None, 1, HEAD_DIM), lambda i: (layer, 0, 0))
    wide = lambda col: pl.BlockSpec((tm, DIFF_WIDTH), lambda i: (i, col // DIFF_WIDTH))
    return pl.pallas_call(
        _qkprep_kernel,
        out_shape=[
            jax.ShapeDtypeStruct((N_TOK, DIFF_WIDTH), BF16),
            jax.ShapeDtypeStruct((N_TOK, DIFF_WIDTH), BF16),
            jax.ShapeDtypeStruct((N_TOK, DIFF_WIDTH), BF16),
            jax.ShapeDtypeStruct((N_TOK, GQA_WIDTH), BF16),
            jax.ShapeDtypeStruct((N_TOK, kv_w), BF16),
        ],
        grid=(N_TOK // tm,),
        in_specs=[
            wide(COL_DQ), wide(COL_DK), wide(COL_GQ),
            pl.BlockSpec((tm, kv_w), lambda i: (i, COL_GK // kv_w)),
            tab_spec, tab_spec, tab_spec, tab_spec, gain_spec, gain_spec,
        ],
        out_specs=[
            pl.BlockSpec((tm, DIFF_WIDTH), lambda i: (i, 0)),
            pl.BlockSpec((tm, DIFF_WIDTH), lambda i: (i, 0)),
            pl.BlockSpec((tm, DIFF_WIDTH), lambda i: (i, 0)),
            pl.BlockSpec((tm, GQA_WIDTH), lambda i: (i, 0)),
            pl.BlockSpec((tm, kv_w), lambda i: (i, 0)),
        ],
        compiler_params=_params(1),
        name="qk_prep",
    )(p, p, p, p, cos_d, sin_d, cos_g, sin_g, q_gain.reshape(-1, 1, HEAD_DIM), k_gain.reshape(-1, 1, HEAD_DIM))


ATT_ROWS = 512
ATT_KCHUNK = 768
LOG2E = 1.4426950408889634


def _assemble_kv(k_refs, v_refs, k_all, v_aug):
    r = 0
    for k, v in zip(k_refs, v_refs):
        n = k.shape[0]
        k_all[r:r + n, :] = k[...]
        v_aug[r:r + n, 0:HEAD_DIM] = v[...]
        r += n
    v_aug[:, HEAD_DIM:] = jnp.ones((v_aug.shape[0], HEAD_DIM), BF16)


def _tile_scores(qq, k_all, s_ref, m_ref):
    s = lax.dot_general(qq, k_all[...], (((1,), (1,)), ((), ())), preferred_element_type=F32)
    s_ref[...] = s
    m_ref[...] = jnp.max(s, axis=-1, keepdims=True)


def _tile_pv(s_ref, m_ref, v_aug):
    m = m_ref[...]
    n_keys = s_ref.shape[1]
    chunk = min(ATT_KCHUNK, n_keys)
    acc = None
    for c in range(0, n_keys, chunk):
        e = jnp.exp2(s_ref[:, c:c + chunk] - m).astype(BF16)
        o = jnp.dot(e, v_aug[c:c + chunk, :], preferred_element_type=F32)
        acc = o if acc is None else acc + o
    return acc[:, :HEAD_DIM] / acc[:, HEAD_DIM:]


def _attn_pipeline(n_tiles, load_q, finish, k_all, v_aug, sa, ma, sb, mb):
    _tile_scores(load_q(0), k_all, sa, ma)
    if n_tiles == 1:
        finish(0, _tile_pv(sa, ma, v_aug))
        return
    assert n_tiles % 2 == 0

    def pair(j, carry):
        t0 = 2 * j
        r0 = _tile_pv(sa, ma, v_aug)
        _tile_scores(load_q(t0 + 1), k_all, sb, mb)
        finish(t0, r0)
        r1 = _tile_pv(sb, mb, v_aug)
        _tile_scores(load_q(jnp.minimum(t0 + 2, n_tiles - 1)), k_all, sa, ma)
        finish(t0 + 1, r1)
        return carry

    lax.fori_loop(0, n_tiles // 2, pair, 0)


def _att_scratch(n_keys):
    return [
        pltpu.VMEM((n_keys, HEAD_DIM), BF16),
        pltpu.VMEM((n_keys, 2 * HEAD_DIM), BF16),
        pltpu.VMEM((ATT_ROWS, n_keys), F32),
        pltpu.VMEM((ATT_ROWS, 1), F32),
        pltpu.VMEM((ATT_ROWS, n_keys), F32),
        pltpu.VMEM((ATT_ROWS, 1), F32),
    ]


def _tile_rows(t, tq):
    return pl.ds(t * tq if isinstance(t, int) else pl.multiple_of(t * tq, tq), tq)


def _diff_attn_kernel(lam_ref, gain_ref, q0_ref, q1_ref, *refs, lam_init, n_blocks):
    k_refs = refs[0:2 * n_blocks:2]
    v_refs = refs[1:2 * n_blocks:2]
    o_ref = refs[2 * n_blocks]
    k_all, v_aug, sa, ma, sb, mb = refs[2 * n_blocks + 1:]
    tq = ATT_ROWS // 2
    n_tiles = q0_ref.shape[0] // tq
    lv = lam_ref[...]
    lam = (jnp.exp(jnp.sum(lv[0:1] * lv[1:2], axis=-1, keepdims=True))
           - jnp.exp(jnp.sum(lv[2:3] * lv[3:4], axis=-1, keepdims=True)) + lam_init)
    gain = gain_ref[...] * (1.0 - lam_init)
    _assemble_kv(k_refs, v_refs, k_all, v_aug)

    def load_q(t):
        return jnp.concatenate([q0_ref[_tile_rows(t, tq), :], q1_ref[_tile_rows(t, tq), :]], axis=0)

    def finish(t, r):
        o = r[:tq] - lam * r[tq:]
        o_ref[_tile_rows(t, tq), :] = (_rms(o) * gain).astype(o_ref.dtype)

    _attn_pipeline(n_tiles, load_q, finish, k_all, v_aug, sa, ma, sb, mb)


def _diff_attention(qd0, qd1, kd, p, lam_vecs, subln, layer_idx, latent):
    lam_init = 0.8 - 0.6 * math.exp(-0.3 * layer_idx)
    v_col = COL_DV // HEAD_DIM
    ctx_blk = N_LAT // CTX_LEN
    kv_specs = [
        pl.BlockSpec((CTX_LEN, HEAD_DIM), lambda b, h: (ctx_blk + b, h)),
        pl.BlockSpec((CTX_LEN, HEAD_DIM), lambda b, h: (ctx_blk + b, v_col + h)),
    ]
    kv_args = (kd, p)
    if latent:
        rows, n_q, n_keys = N_LAT, SEQ, N_KEYS
        q_spec = pl.BlockSpec((SEQ, HEAD_DIM), lambda b, h: (b, h))
        kv_specs += [
            pl.BlockSpec((SEQ, HEAD_DIM), lambda b, h: (b, h)),
            pl.BlockSpec((SEQ, HEAD_DIM), lambda b, h: (b, v_col + h)),
        ]
        kv_args += (kd, p)
    else:
        rows, n_q, n_keys = N_CTX, CTX_LEN, CTX_LEN
        q_spec = pl.BlockSpec((CTX_LEN, HEAD_DIM), lambda b, h: (ctx_blk + b, h))
    return pl.pallas_call(
        functools.partial(_diff_attn_kernel, lam_init=lam_init, n_blocks=len(kv_args) // 2),
        out_shape=jax.ShapeDtypeStruct((rows, DIFF_WIDTH), BF16),
        grid=(BATCH, DIFF_HEADS),
        in_specs=[
            pl.BlockSpec((None, 4, DIFF_QK_DIM), lambda b, h: (layer_idx, 0, 0)),
            pl.BlockSpec((None, 1, HEAD_DIM), lambda b, h: (layer_idx, 0, 0)),
            q_spec, q_spec, *kv_specs,
        ],
        out_specs=pl.BlockSpec((n_q, HEAD_DIM), lambda b, h: (b, h)),
        scratch_shapes=_att_scratch(n_keys),
        compiler_params=_params(2),
        name="diff_attention_lat" if latent else "diff_attention_ctx",
    )(lam_vecs, subln.reshape(-1, 1, HEAD_DIM), qd0, qd1, *kv_args)


def _gqa_attn_kernel(q_ref, *refs, n_blocks):
    k_refs = refs[0:2 * n_blocks:2]
    v_refs = refs[1:2 * n_blocks:2]
    o_ref = refs[2 * n_blocks]
    k_all, v_aug, sa, ma, sb, mb = refs[2 * n_blocks + 1:]
    tq = ATT_ROWS // GQA_GROUP
    n_tiles = q_ref.shape[0] // tq
    _assemble_kv(k_refs, v_refs, k_all, v_aug)

    def load_q(t):
        rows = _tile_rows(t, tq)
        return jnp.concatenate([q_ref[rows, g * HEAD_DIM:(g + 1) * HEAD_DIM] for g in range(GQA_GROUP)], axis=0)

    def finish(t, r):
        rows = _tile_rows(t, tq)
        for g in range(GQA_GROUP):
            o_ref[rows, g * HEAD_DIM:(g + 1) * HEAD_DIM] = r[g * tq:(g + 1) * tq].astype(o_ref.dtype)

    _attn_pipeline(n_tiles, load_q, finish, k_all, v_aug, sa, ma, sb, mb)


def _gqa_attention(qg, kg, p, latent):
    v_col = COL_GV // HEAD_DIM
    ctx_blk = N_LAT // CTX_LEN
    gw = GQA_GROUP * HEAD_DIM
    kv_specs = [
        pl.BlockSpec((CTX_LEN, HEAD_DIM), lambda b, g: (ctx_blk + b, g)),
        pl.BlockSpec((CTX_LEN, HEAD_DIM), lambda b, g: (ctx_blk + b, v_col + g)),
    ]
    kv_args = (kg, p)
    if latent:
        rows, n_q, n_keys = N_LAT, SEQ, N_KEYS
        q_spec = pl.BlockSpec((SEQ, gw), lambda b, g: (b, g))
        kv_specs += [
            pl.BlockSpec((SEQ, HEAD_DIM), lambda b, g: (b, g)),
            pl.BlockSpec((SEQ, HEAD_DIM), lambda b, g: (b, v_col + g)),
        ]
        kv_args += (kg, p)
    else:
        rows, n_q, n_keys = N_CTX, CTX_LEN, CTX_LEN
        q_spec = pl.BlockSpec((CTX_LEN, gw), lambda b, g: (ctx_blk + b, g))
    return pl.pallas_call(
        functools.partial(_gqa_attn_kernel, n_blocks=len(kv_args) // 2),
        out_shape=jax.ShapeDtypeStruct((rows, GQA_WIDTH), BF16),
        grid=(BATCH, GQA_KV_HEADS),
        in_specs=[q_spec, *kv_specs],
        out_specs=pl.BlockSpec((n_q, gw), lambda b, g: (b, g)),
        scratch_shapes=_att_scratch(n_keys),
        compiler_params=_params(2),
        name="gqa_attention_lat" if latent else "gqa_attention_ctx",
    )(qg, *kv_args)


LRU_SLAB = 256
LRU_ROWS = 256


def _conv_centred(x, w, bias):
    n = x.shape[0]
    row = lax.broadcasted_iota(jnp.int32, x.shape, 0)
    xm1 = jnp.where(row >= 1, pltpu.roll(x, 1, 0), 0.0)
    xp1 = jnp.where(row < n - 1, pltpu.roll(x, n - 1, 0), 0.0)
    xp2 = jnp.where(row < n - 2, pltpu.roll(x, n - 2, 0), 0.0)
    return xm1 * w[0:1] + x * w[1:2] + xp1 * w[2:3] + xp2 * w[3:4] + bias


def _gelu_tanh(g):
    return 0.5 * g * (1.0 + jnp.tanh(0.7978845608028654 * (g + 0.044715 * (g * g * g))))


def _lru_kernel(*refs, need_ctx):
    if need_ctx:
        (xl_ref, xc_ref, gl_ref, gc_ref, cw_ref, cb_ref, wa_ref, ba_ref, wi_ref, bi_ref, al_ref,
         yl_ref, yc_ref, xconv_ref, a_ref, u_ref, hf_ref, hb_ref) = refs
    else:
        (xl_ref, xc_ref, gl_ref, cw_ref, cb_ref, wa_ref, ba_ref, wi_ref, bi_ref, al_ref,
         yl_ref, xconv_ref, a_ref, u_ref, hf_ref, hb_ref) = refs
    cw = cw_ref[...]
    cb = cb_ref[...]
    xconv_ref[0:CTX_LEN, :] = _conv_centred(xc_ref[...].astype(F32), cw, cb)
    xconv_ref[CTX_LEN:N_KEYS, :] = _conv_centred(xl_ref[...].astype(F32), cw, cb)

    z = -al_ref[...]
    sp = jnp.maximum(z, 0.0) + jnp.log1p(jnp.exp(-jnp.abs(z)))
    ba = ba_ref[...]
    bi = bi_ref[...]

    def gate_chunk(ci, carry):
        r0 = pl.multiple_of(ci * LRU_ROWS, LRU_ROWS)
        xc = xconv_ref[pl.ds(r0, LRU_ROWS), :]
        xb = xc.astype(BF16)
        for g in range(LRU_SLAB // LRU_BLOCK):
            sl = slice(g * LRU_BLOCK, (g + 1) * LRU_BLOCK)
            for d in range(2):
                ra = jnp.dot(xb[:, sl], wa_ref[d, g].astype(BF16), preferred_element_type=F32) + ba[d:d + 1, sl]
                ri = jnp.dot(xb[:, sl], wi_ref[d, g].astype(BF16), preferred_element_type=F32) + bi[d:d + 1, sl]
                log_a = (-LRU_C * jax.nn.sigmoid(ra)) * sp[d:d + 1, sl]
                a = jnp.exp(log_a)
                u = jnp.sqrt(1.0 - a * a) * (jax.nn.sigmoid(ri) * xc[:, sl])
                a_ref[d, pl.ds(r0, LRU_ROWS), sl] = a
                u_ref[d, pl.ds(r0, LRU_ROWS), sl] = u
        return carry

    lax.fori_loop(0, N_KEYS // LRU_ROWS, gate_chunk, 0)

    row8 = lax.broadcasted_iota(jnp.int32, (SUBLANES, LRU_SLAB), 0)

    def block_scan(a, u, h_in, reverse):
        for k in (1, 2, 4):
            if reverse:
                keep = row8 < SUBLANES - k
                shift = SUBLANES - k
            else:
                keep = row8 >= k
                shift = k
            a_s = jnp.where(keep, pltpu.roll(a, shift, 0), 1.0)
            u_s = jnp.where(keep, pltpu.roll(u, shift, 0), 0.0)
            u = a * u_s + u
            a = a * a_s
        h = a * h_in + u
        last = h[0:1, :] if reverse else h[SUBLANES - 1:SUBLANES, :]
        return h, jnp.broadcast_to(last, h.shape)

    def scan_pair(base, n_blocks):
        def body(i, carry):
            hf, hb = carry
            rf = pl.multiple_of(base + i * SUBLANES, SUBLANES)
            rb = pl.multiple_of(base + (n_blocks - 1 - i) * SUBLANES, SUBLANES)
            h, hf = block_scan(a_ref[0, pl.ds(rf, SUBLANES), :], u_ref[0, pl.ds(rf, SUBLANES), :], hf, False)
            hf_ref[pl.ds(rf, SUBLANES), :] = h
            h, hb = block_scan(a_ref[1, pl.ds(rb, SUBLANES), :], u_ref[1, pl.ds(rb, SUBLANES), :], hb, True)
            hb_ref[pl.ds(rb, SUBLANES), :] = h
            return hf, hb
        return body

    zero = jnp.zeros((SUBLANES, LRU_SLAB), F32)
    carry = lax.fori_loop(0, CTX_LEN // SUBLANES, scan_pair(0, CTX_LEN // SUBLANES), (zero, zero), unroll=2)
    lax.fori_loop(0, SEQ // SUBLANES, scan_pair(CTX_LEN, SEQ // SUBLANES), carry, unroll=2)

    def out_chunk(ci, carry):
        r0 = pl.multiple_of(ci * LRU_ROWS, LRU_ROWS)
        h = hf_ref[pl.ds(CTX_LEN + r0, LRU_ROWS), :] + hb_ref[pl.ds(CTX_LEN + r0, LRU_ROWS), :]
        yl_ref[pl.ds(r0, LRU_ROWS), :] = (h * _gelu_tanh(gl_ref[pl.ds(r0, LRU_ROWS), :].astype(F32))).astype(BF16)
        return carry

    lax.fori_loop(0, SEQ // LRU_ROWS, out_chunk, 0)
    if need_ctx:
        h = hf_ref[0:CTX_LEN, :] + hb_ref[0:CTX_LEN, :]
        yc_ref[...] = (h * _gelu_tanh(gc_ref[...].astype(F32))).astype(BF16)


def _rglru(p, conv_w, conv_b, w_a, b_a, w_i, b_i, a_logit, layer, need_ctx):
    slab = LRU_SLAB
    n_slab = LRU_WIDTH // slab
    gpb = slab // LRU_BLOCK
    ctx_blk = N_LAT // CTX_LEN
    x_col = COL_RX // slab
    g_col = COL_RG // slab
    in_specs = [
        pl.BlockSpec((SEQ, slab), lambda b, s: (b, x_col + s)),
        pl.BlockSpec((CTX_LEN, slab), lambda b, s: (ctx_blk + b, x_col + s)),
        pl.BlockSpec((SEQ, slab), lambda b, s: (b, g_col + s)),
    ]
    args = [p, p, p]
    if need_ctx:
        in_specs.append(pl.BlockSpec((CTX_LEN, slab), lambda b, s: (ctx_blk + b, g_col + s)))
        args.append(p)
    in_specs += [
        pl.BlockSpec((None, CONV_WIDTH, slab), lambda b, s: (layer, 0, s)),
        pl.BlockSpec((None, 1, slab), lambda b, s: (layer, 0, s)),
        pl.BlockSpec((None, 2, gpb, LRU_BLOCK, LRU_BLOCK), lambda b, s: (layer, 0, s, 0, 0)),
        pl.BlockSpec((None, 2, slab), lambda b, s: (layer, 0, s)),
        pl.BlockSpec((None, 2, gpb, LRU_BLOCK, LRU_BLOCK), lambda b, s: (layer, 0, s, 0, 0)),
        pl.BlockSpec((None, 2, slab), lambda b, s: (layer, 0, s)),
        pl.BlockSpec((None, 2, slab), lambda b, s: (layer, 0, s)),
    ]
    args += [conv_w, conv_b.reshape(-1, 1, LRU_WIDTH), w_a, b_a, w_i, b_i, a_logit]
    out_shape = [jax.ShapeDtypeStruct((N_LAT, LRU_WIDTH), BF16)]
    out_specs = [pl.BlockSpec((SEQ, slab), lambda b, s: (b, s))]
    if need_ctx:
        out_shape.append(jax.ShapeDtypeStruct((N_CTX, LRU_WIDTH), BF16))
        out_specs.append(pl.BlockSpec((CTX_LEN, slab), lambda b, s: (b, s)))
    return pl.pallas_call(
        functools.partial(_lru_kernel, need_ctx=need_ctx),
        out_shape=out_shape,
        grid=(BATCH, n_slab),
        in_specs=in_specs,
        out_specs=out_specs,
        scratch_shapes=[
            pltpu.VMEM((N_KEYS, slab), F32),
            pltpu.VMEM((2, N_KEYS, slab), F32),
            pltpu.VMEM((2, N_KEYS, slab), F32),
            pltpu.VMEM((N_KEYS, slab), F32),
            pltpu.VMEM((N_KEYS, slab), F32),
        ],
        compiler_params=_params(2),
        name="rglru",
    )(*args)


def _outproj_kernel(d_ref, g_ref, l_ref, w_ref, *refs, n_first):
    *x_refs, gate_ref, o_ref, wb_ref = refs

    @pl.when(pl.program_id(1) == 0)
    def _():
        wb_ref[...] = w_ref[...].astype(BF16)

    acc = jnp.dot(d_ref[...], wb_ref[0:DIFF_WIDTH, :], preferred_element_type=F32)
    acc += jnp.dot(g_ref[...], wb_ref[DIFF_WIDTH:DIFF_WIDTH + GQA_WIDTH, :], preferred_element_type=F32)
    acc += jnp.dot(l_ref[...], wb_ref[DIFF_WIDTH + GQA_WIDTH:, :], preferred_element_type=F32)
    o_ref[...] = _tile_source(pl.program_id(1), n_first, x_refs) + gate_ref[...] * acc


def _out_projection(diff, gqa, lru, w_out, xs, mods, layer, rows):
    tm, tn = 1024, 512
    n_first = N_LAT // tm
    return pl.pallas_call(
        functools.partial(_outproj_kernel, n_first=n_first),
        out_shape=jax.ShapeDtypeStruct((rows, D_MODEL), F32),
        grid=(D_MODEL // tn, rows // tm),
        in_specs=[
            pl.BlockSpec((tm, DIFF_WIDTH), lambda j, i: (i, 0)),
            pl.BlockSpec((tm, GQA_WIDTH), lambda j, i: (i, 0)),
            pl.BlockSpec((tm, LRU_WIDTH), lambda j, i: (i, 0)),
            pl.BlockSpec((None, D_MODEL, tn), lambda j, i: (layer, 0, j)),
            *_split_specs((tm, tn), n_first, 1, 0, len(xs)),
            _mod_spec(layer, 2, tm, tn, 1, 0),
        ],
        out_specs=pl.BlockSpec((tm, tn), lambda j, i: (i, j)),
        scratch_shapes=[pltpu.VMEM((D_MODEL, tn), BF16)],
        compiler_params=_params(2),
        name="out_projection",
    )(diff, gqa, lru, w_out, *xs, mods)


MLP_TM = 512
MLP_TF = 512
MLP_TN = 1024


def _mlp_kernel(x_ref, sh_ref, sc_ref, gate_ref, w1_ref, w2_ref, fin_ref, o_ref, h_ref, *, final):
    f = pl.program_id(1)

    @pl.when(f == 0)
    def _():
        y = _rms(x_ref[...])
        h_ref[...] = (y * (1.0 + sc_ref[...]) + sh_ref[...]).astype(BF16)
        o_ref[...] = jnp.zeros_like(o_ref)

    t = jnp.dot(h_ref[...], w1_ref[...], preferred_element_type=F32)
    t = jnp.maximum(t, 0.0)
    tb = (t * t).astype(BF16)

    for c in range(D_MODEL // MLP_TN):
        sl = slice(c * MLP_TN, (c + 1) * MLP_TN)
        o_ref[:, sl] += jnp.dot(tb, w2_ref[:, sl], preferred_element_type=F32)

    @pl.when(f == pl.num_programs(1) - 1)
    def _():
        y = x_ref[...] + gate_ref[...] * o_ref[...]
        if final:
            y = _rms(y) * fin_ref[...]
        o_ref[...] = y


def _mlp(x, mods, w1b, w2b, final_norm, layer, rows, final):
    tm, tf = MLP_TM, MLP_TF
    return pl.pallas_call(
        functools.partial(_mlp_kernel, final=final),
        out_shape=jax.ShapeDtypeStruct((rows, D_MODEL), F32),
        grid=(rows // tm, D_FF // tf),
        in_specs=[
            pl.BlockSpec((tm, D_MODEL), lambda i, f: (i, 0), pipeline_mode=pl.Buffered(1)),
            _mod_spec(layer, 3, tm, D_MODEL, 0, None),
            _mod_spec(layer, 4, tm, D_MODEL, 0, None),
            _mod_spec(layer, 5, tm, D_MODEL, 0, None),
            pl.BlockSpec((None, D_MODEL, tf), lambda i, f: (layer, 0, f)),
            pl.BlockSpec((None, tf, D_MODEL), lambda i, f: (layer, f, 0)),
            pl.BlockSpec((1, D_MODEL), lambda i, f: (0, 0)),
        ],
        out_specs=pl.BlockSpec((tm, D_MODEL), lambda i, f: (i, 0)),
        scratch_shapes=[pltpu.VMEM((tm, D_MODEL), BF16)],
        compiler_params=_params(2),
        name="mlp_final" if final else "mlp",
    )(x, mods, mods, mods, w1b, w2b, final_norm.reshape(1, D_MODEL))


def kernel(x, c, ctx, c_ctx, w_mod, b_mod, w_in, diff_lambda, diff_subln, gqa_q_norm, gqa_k_norm,
           lru_conv_w, lru_conv_b, lru_w_a, lru_b_a, lru_w_i, lru_b_i, lru_a_logit,
           w_out, w_ff1, w_ff2, final_norm):
    depth = w_mod.shape[0]
    c8 = jnp.concatenate([c, c_ctx[None], jnp.zeros((8 - BATCH - 1, D_MODEL), F32)], axis=0)
    mods = _modulation(c8, w_mod, b_mod).reshape(depth * 8 * N_MOD, 1, D_MODEL)
    tables = _rope_tables(DIFF_QK_DIM) + _rope_tables(HEAD_DIM)

    w1b = w_ff1.astype(BF16)
    w2b = w_ff2.astype(BF16)
    xs = (x.reshape(N_LAT, D_MODEL), ctx.reshape(N_CTX, D_MODEL))
    for l in range(depth):
        need_ctx = l < depth - 1
        rows = N_TOK if need_ctx else N_LAT
        h = _norm_mod(xs, mods, l)
        p = _in_projection(h, w_in, l)
        qd0, qd1, kd, qg, kg = _qk_prep(p, gqa_q_norm, gqa_k_norm, tables, l)
        diff = _diff_attention(qd0, qd1, kd, p, diff_lambda, diff_subln, l, True)
        gqa = _gqa_attention(qg, kg, p, True)
        lru = _rglru(p, lru_conv_w, lru_conv_b, lru_w_a, lru_b_a, lru_w_i, lru_b_i, lru_a_logit, l, need_ctx)
        if need_ctx:
            diff = jnp.concatenate([diff, _diff_attention(qd0, qd1, kd, p, diff_lambda, diff_subln, l, False)], 0)
            gqa = jnp.concatenate([gqa, _gqa_attention(qg, kg, p, False)], 0)
            lru = jnp.concatenate(lru, 0)
        else:
            lru = lru[0]
        xt = _out_projection(diff, gqa, lru, w_out, xs, mods, l, rows)
        xs = (_mlp(xt, mods, w1b, w2b, final_norm, l, rows, not need_ctx),)
    return xs[0].reshape(BATCH, SEQ, D_MODEL)
```

```python
import functools
import math

import jax
import jax.numpy as jnp
from jax import lax
from jax.experimental import pallas as pl
from jax.experimental.pallas import tpu as pltpu

D_MODEL = 4096
BATCH = 4
SEQ = 2048
DEPTH = 2
GRID_W = 64
CTX_LEN = 256
HEAD_DIM = 128
DIFF_WIDTH = D_MODEL // 4
DIFF_HEADS = DIFF_WIDTH // HEAD_DIM
DIFF_QK_DIM = HEAD_DIM // 2
GQA_WIDTH = D_MODEL // 4
GQA_HEADS = GQA_WIDTH // HEAD_DIM
GQA_KV_HEADS = 2
GQA_GROUP = GQA_HEADS // GQA_KV_HEADS
LRU_WIDTH = D_MODEL // 2
LRU_BLOCK = 128
LRU_BLOCKS = LRU_WIDTH // LRU_BLOCK
CONV_WIDTH = 4
LRU_C = 8.0
D_FF = 4 * D_MODEL
IN_COLS = 3 * DIFF_WIDTH + GQA_WIDTH + 2 * GQA_KV_HEADS * HEAD_DIM + 2 * LRU_WIDTH
ROPE_THETA = 10000.0
EPS = 1e-6
N_MOD = 6

N_LAT = BATCH * SEQ
N_CTX = BATCH * CTX_LEN
N_TOK = N_LAT + N_CTX
N_KEYS = CTX_LEN + SEQ

COL_DQ = 0
COL_DK = COL_DQ + DIFF_WIDTH
COL_DV = COL_DK + DIFF_WIDTH
COL_GQ = COL_DV + DIFF_WIDTH
COL_GK = COL_GQ + GQA_WIDTH
COL_GV = COL_GK + GQA_KV_HEADS * HEAD_DIM
COL_RX = COL_GV + GQA_KV_HEADS * HEAD_DIM
COL_RG = COL_RX + LRU_WIDTH

LANES = 128
SUBLANES = 8
VMEM_LIMIT = 56 * 1024 * 1024

F32 = jnp.float32
BF16 = jnp.bfloat16


def _params(n_axes, vmem=VMEM_LIMIT):
    return pltpu.CompilerParams(dimension_semantics=("arbitrary",) * n_axes, vmem_limit_bytes=vmem)


def _rms(x):
    return x * lax.rsqrt(jnp.mean(x * x, axis=-1, keepdims=True) + EPS)


def _mod_row(tile, tm):
    return jnp.minimum((tile * tm) // SEQ, BATCH)


def _mod_kernel(c_ref, w_ref, b_ref, o_ref):
    c = c_ref[...]
    s = (c * jax.nn.sigmoid(c)).astype(BF16)
    o_ref[...] = jnp.dot(s, w_ref[...].astype(BF16), preferred_element_type=F32) + b_ref[...]


def _modulation(c8, w_mod, b_mod):
    depth = w_mod.shape[0]
    tn = 512
    return pl.pallas_call(
        _mod_kernel,
        out_shape=jax.ShapeDtypeStruct((depth, 8, N_MOD * D_MODEL), F32),
        grid=(depth, N_MOD * D_MODEL // tn),
        in_specs=[
            pl.BlockSpec((8, D_MODEL), lambda l, j: (0, 0)),
            pl.BlockSpec((None, D_MODEL, tn), lambda l, j: (l, 0, j)),
            pl.BlockSpec((None, 1, tn), lambda l, j: (l, 0, j)),
        ],
        out_specs=pl.BlockSpec((None, 8, tn), lambda l, j: (l, 0, j)),
        compiler_params=_params(2),
        name="modulation",
    )(c8, w_mod, b_mod.reshape(depth, 1, N_MOD * D_MODEL))


def _mod_spec(layer, which, tm, tn, row_axis, col_axis):
    def index_map(*ids):
        row = _mod_row(ids[row_axis], tm)
        col = 0 if col_axis is None else ids[col_axis]
        return (layer * 8 * N_MOD + row * N_MOD + which, 0, col)
    return pl.BlockSpec((None, 1, tn), index_map)


def _tile_source(i, n_first, refs):
    if len(refs) == 1:
        return refs[0][...]
    return jnp.where(i < n_first, refs[0][...], refs[1][...])


def _split_specs(block, n_first, tile_axis, col_axis, n_sources):
    def col(ids):
        return 0 if col_axis is None else ids[col_axis]
    if n_sources == 1:
        return [pl.BlockSpec(block, lambda *ids: (ids[tile_axis], col(ids)))]
    return [
        pl.BlockSpec(block, lambda *ids: (jnp.minimum(ids[tile_axis], n_first - 1), col(ids))),
        pl.BlockSpec(block, lambda *ids: (jnp.maximum(ids[tile_axis] - n_first, 0), col(ids))),
    ]


def _norm_mod_kernel(*refs, n_first):
    *x_refs, sh_ref, sc_ref, o_ref = refs
    y = _rms(_tile_source(pl.program_id(0), n_first, x_refs))
    o_ref[...] = (y * (1.0 + sc_ref[...]) + sh_ref[...]).astype(BF16)


def _norm_mod(xs, mods, layer):
    tm = 256
    n_first = N_LAT // tm
    return pl.pallas_call(
        functools.partial(_norm_mod_kernel, n_first=n_first),
        out_shape=jax.ShapeDtypeStruct((N_TOK, D_MODEL), BF16),
        grid=(N_TOK // tm,),
        in_specs=[
            *_split_specs((tm, D_MODEL), n_first, 0, None, len(xs)),
            _mod_spec(layer, 0, tm, D_MODEL, 0, None),
            _mod_spec(layer, 1, tm, D_MODEL, 0, None),
        ],
        out_specs=pl.BlockSpec((tm, D_MODEL), lambda i: (i, 0)),
        compiler_params=_params(1),
        name="norm_mod",
    )(*xs, mods, mods)


CAST_BLOCKS = 128


def _inproj_kernel(h_ref, w_ref, f1_ref, f2_ref, o_ref, c1_ref, c2_ref, wb_ref):
    @pl.when(pl.program_id(1) == 0)
    def _():
        wb_ref[...] = w_ref[...].astype(BF16)

    o_ref[...] = jnp.dot(h_ref[...], wb_ref[...], preferred_element_type=F32).astype(o_ref.dtype)

    for c in range(D_FF // MLP_TF):
        c1_ref[c] = f1_ref[:, c * MLP_TF:(c + 1) * MLP_TF].astype(BF16)
    c2_ref[...] = f2_ref[...].astype(BF16)


def _in_projection(h, w_in, w_ff1, w_ff2, layer):
    tm, tn = 1024, 512
    n_i = N_TOK // tm
    assert (IN_COLS // tn) * n_i >= CAST_BLOCKS
    r1 = D_MODEL // CAST_BLOCKS
    r2 = D_FF // CAST_BLOCKS

    def cast_block(j, i):
        return jnp.minimum(j * n_i + i, CAST_BLOCKS - 1)

    return pl.pallas_call(
        _inproj_kernel,
        out_shape=[
            jax.ShapeDtypeStruct((N_TOK, IN_COLS), BF16),
            jax.ShapeDtypeStruct((D_FF // MLP_TF, D_MODEL, MLP_TF), BF16),
            jax.ShapeDtypeStruct((D_FF, D_MODEL), BF16),
        ],
        grid=(IN_COLS // tn, n_i),
        in_specs=[
            pl.BlockSpec((tm, D_MODEL), lambda j, i: (i, 0)),
            pl.BlockSpec((None, D_MODEL, tn), lambda j, i: (layer, 0, j)),
            pl.BlockSpec((None, r1, D_FF), lambda j, i: (layer, cast_block(j, i), 0)),
            pl.BlockSpec((None, r2, D_MODEL), lambda j, i: (layer, cast_block(j, i), 0)),
        ],
        out_specs=[
            pl.BlockSpec((tm, tn), lambda j, i: (i, j)),
            pl.BlockSpec((D_FF // MLP_TF, r1, MLP_TF), lambda j, i: (0, cast_block(j, i), 0)),
            pl.BlockSpec((r2, D_MODEL), lambda j, i: (cast_block(j, i), 0)),
        ],
        scratch_shapes=[pltpu.VMEM((D_MODEL, tn), BF16)],
        compiler_params=_params(2),
        name="in_projection",
    )(h, w_in, w_ff1, w_ff2)


def _rope_tables(dim):
    n_freq = dim // 4
    pos = jnp.arange(SEQ)
    row = (pos // GRID_W).astype(F32)
    col = (pos % GRID_W).astype(F32)
    inv = ROPE_THETA ** (-jnp.arange(n_freq, dtype=F32) / n_freq)
    ang_r = row[:, None] * inv
    ang_c = col[:, None] * inv
    cos = jnp.concatenate([jnp.cos(ang_r), jnp.cos(ang_r), jnp.cos(ang_c), jnp.cos(ang_c)], axis=1)
    sin = jnp.concatenate([-jnp.sin(ang_r), jnp.sin(ang_r), -jnp.sin(ang_c), jnp.sin(ang_c)], axis=1)
    reps = LANES // dim
    cos = jnp.tile(cos, (1, reps))
    sin = jnp.tile(sin, (1, reps))
    cos = jnp.concatenate([cos, jnp.ones((CTX_LEN, LANES), F32)], axis=0)
    sin = jnp.concatenate([sin, jnp.zeros((CTX_LEN, LANES), F32)], axis=0)
    return cos, sin


def _rope(x, cos, sin, first, half):
    partner = jnp.where(first, pltpu.roll(x, LANES - half, 1), pltpu.roll(x, half, 1))
    return x * cos + partner * sin


def _qkprep_kernel(dq_ref, dk_ref, gq_ref, gk_ref, cd_ref, sd_ref, cg_ref, sg_ref, qgain_ref, kgain_ref,
                   qd0_ref, qd1_ref, kd_ref, qg_ref, kg_ref):
    cd, sd, cg, sg = cd_ref[...], sd_ref[...], cg_ref[...], sg_ref[...]
    qgain, kgain = qgain_ref[...], kgain_ref[...]
    lane = lax.broadcasted_iota(jnp.int32, cd.shape, 1)
    first_d = (lane & (DIFF_QK_DIM // 2 - 1)) < DIFF_QK_DIM // 4
    first_g = (lane & (HEAD_DIM // 2 - 1)) < HEAD_DIM // 4
    map0 = lane < DIFF_QK_DIM
    d_scale = DIFF_QK_DIM ** -0.5 * LOG2E
    g_scale = HEAD_DIM ** -0.5 * LOG2E
    for c in range(DIFF_WIDTH // LANES):
        sl = slice(c * LANES, (c + 1) * LANES)
        q = _rope(dq_ref[:, sl].astype(F32), cd, sd, first_d, DIFF_QK_DIM // 4) * d_scale
        qd0_ref[:, sl] = jnp.where(map0, q, 0.0).astype(BF16)
        qd1_ref[:, sl] = jnp.where(map0, 0.0, q).astype(BF16)
        kd_ref[:, sl] = _rope(dk_ref[:, sl].astype(F32), cd, sd, first_d, DIFF_QK_DIM // 4).astype(BF16)
    for c in range(GQA_WIDTH // LANES):
        sl = slice(c * LANES, (c + 1) * LANES)
        qn = _rms(gq_ref[:, sl].astype(F32)) * qgain
        qg_ref[:, sl] = (_rope(qn, cg, sg, first_g, HEAD_DIM // 4) * g_scale).astype(BF16)
    for c in range(GQA_KV_HEADS):
        sl = slice(c * LANES, (c + 1) * LANES)
        kn = _rms(gk_ref[:, sl].astype(F32)) * kgain
        kg_ref[:, sl] = _rope(kn, cg, sg, first_g, HEAD_DIM // 4).astype(BF16)


def _qk_prep(p, q_gain, k_gain, tables, layer):
    tm = 256
    cos_d, sin_d, cos_g, sin_g = tables
    lat_tiles = N_LAT // tm

    def pos_block(i):
        return jnp.where(i < lat_tiles, i % (SEQ // tm), SEQ // tm)

    kv_w = GQA_KV_HEADS * HEAD_DIM
    tab_spec = pl.BlockSpec((tm, LANES), lambda i: (pos_block(i), 0))
    gain_spec = pl.BlockSpec((None, 1, HEAD_DIM), lambda i: (layer, 0, 0))
    wide = lambda col: pl.BlockSpec((tm, DIFF_WIDTH), lambda i: (i, col // DIFF_WIDTH))
    return pl.pallas_call(
        _qkprep_kernel,
        out_shape=[
            jax.ShapeDtypeStruct((N_TOK, DIFF_WIDTH), BF16),
            jax.ShapeDtypeStruct((N_TOK, DIFF_WIDTH), BF16),
            jax.ShapeDtypeStruct((N_TOK, DIFF_WIDTH), BF16),
            jax.ShapeDtypeStruct((N_TOK, GQA_WIDTH), BF16),
            jax.ShapeDtypeStruct((N_TOK, kv_w), BF16),
        ],
        grid=(N_TOK // tm,),
        in_specs=[
            wide(COL_DQ), wide(COL_DK), wide(COL_GQ),
            pl.BlockSpec((tm, kv_w), lambda i: (i, COL_GK // kv_w)),
            tab_spec, tab_spec, tab_spec, tab_spec, gain_spec, gain_spec,
        ],
        out_specs=[
            pl.BlockSpec((tm, DIFF_WIDTH), lambda i: (i, 0)),
            pl.BlockSpec((tm, DIFF_WIDTH), lambda i: (i, 0)),
            pl.BlockSpec((tm, DIFF_WIDTH), lambda i: (i, 0)),
            pl.BlockSpec((tm, GQA_WIDTH), lambda i: (i, 0)),
            pl.BlockSpec((tm, kv_w), lambda i: (i, 0)),
        ],
        compiler_params=_params(1),
        name="qk_prep",
    )(p, p, p, p, cos_d, sin_d, cos_g, sin_g, q_gain.reshape(-1, 1, HEAD_DIM), k_gain.reshape(-1, 1, HEAD_DIM))


ATT_ROWS = 512
ATT_KCHUNK = 768
LOG2E = 1.4426950408889634


def _assemble_kv(k_refs, v_refs, k_all, v_aug):
    r = 0
    for k, v in zip(k_refs, v_refs):
        n = k.shape[0]
        k_all[r:r + n, :] = k[...]
        v_aug[r:r + n, 0:HEAD_DIM] = v[...]
        r += n
    v_aug[:, HEAD_DIM:] = jnp.ones((v_aug.shape[0], HEAD_DIM), BF16)


def _tile_scores(qq, k_all, s_ref, m_ref):
    s = lax.dot_general(qq, k_all[...], (((1,), (1,)), ((), ())), preferred_element_type=F32)
    s_ref[...] = s
    m_ref[...] = jnp.max(s, axis=-1, keepdims=True)


def _tile_pv(s_ref, m_ref, v_aug):
    m = m_ref[...]
    n_keys = s_ref.shape[1]
    chunk = min(ATT_KCHUNK, n_keys)
    acc = None
    for c in range(0, n_keys, chunk):
        e = jnp.exp2(s_ref[:, c:c + chunk] - m).astype(BF16)
        o = jnp.dot(e, v_aug[c:c + chunk, :], preferred_element_type=F32)
        acc = o if acc is None else acc + o
    return acc[:, :HEAD_DIM] / acc[:, HEAD_DIM:]


def _attn_pipeline(n_tiles, load_q, finish, k_all, v_aug, sa, ma, sb, mb):
    _tile_scores(load_q(0), k_all, sa, ma)
    if n_tiles == 1:
        finish(0, _tile_pv(sa, ma, v_aug))
        return
    assert n_tiles % 2 == 0

    def pair(j, carry):
        t0 = 2 * j
        r0 = _tile_pv(sa, ma, v_aug)
        _tile_scores(load_q(t0 + 1), k_all, sb, mb)
        finish(t0, r0)
        r1 = _tile_pv(sb, mb, v_aug)
        _tile_scores(load_q(jnp.minimum(t0 + 2, n_tiles - 1)), k_all, sa, ma)
        finish(t0 + 1, r1)
        return carry

    lax.fori_loop(0, n_tiles // 2, pair, 0)


def _att_scratch(n_keys):
    return [
        pltpu.VMEM((n_keys, HEAD_DIM), BF16),
        pltpu.VMEM((n_keys, 2 * HEAD_DIM), BF16),
        pltpu.VMEM((ATT_ROWS, n_keys), F32),
        pltpu.VMEM((ATT_ROWS, 1), F32),
        pltpu.VMEM((ATT_ROWS, n_keys), F32),
        pltpu.VMEM((ATT_ROWS, 1), F32),
    ]


def _tile_rows(t, tq):
    return pl.ds(t * tq if isinstance(t, int) else pl.multiple_of(t * tq, tq), tq)


def _diff_attn_kernel(lam_ref, gain_ref, q0_ref, q1_ref, *refs, lam_init, n_blocks):
    k_refs = refs[0:2 * n_blocks:2]
    v_refs = refs[1:2 * n_blocks:2]
    o_ref = refs[2 * n_blocks]
    k_all, v_aug, sa, ma, sb, mb = refs[2 * n_blocks + 1:]
    tq = ATT_ROWS // 2
    n_tiles = q0_ref.shape[0] // tq
    lv = lam_ref[...]
    lam = (jnp.exp(jnp.sum(lv[0:1] * lv[1:2], axis=-1, keepdims=True))
           - jnp.exp(jnp.sum(lv[2:3] * lv[3:4], axis=-1, keepdims=True)) + lam_init)
    gain = gain_ref[...] * (1.0 - lam_init)
    _assemble_kv(k_refs, v_refs, k_all, v_aug)

    def load_q(t):
        return jnp.concatenate([q0_ref[_tile_rows(t, tq), :], q1_ref[_tile_rows(t, tq), :]], axis=0)

    def finish(t, r):
        o = r[:tq] - lam * r[tq:]
        o_ref[_tile_rows(t, tq), :] = (_rms(o) * gain).astype(o_ref.dtype)

    _attn_pipeline(n_tiles, load_q, finish, k_all, v_aug, sa, ma, sb, mb)


def _diff_attention(qd0, qd1, kd, p, lam_vecs, subln, layer_idx, latent):
    lam_init = 0.8 - 0.6 * math.exp(-0.3 * layer_idx)
    v_col = COL_DV // HEAD_DIM
    ctx_blk = N_LAT // CTX_LEN
    kv_specs = [
        pl.BlockSpec((CTX_LEN, HEAD_DIM), lambda b, h: (ctx_blk + b, h)),
        pl.BlockSpec((CTX_LEN, HEAD_DIM), lambda b, h: (ctx_blk + b, v_col + h)),
    ]
    kv_args = (kd, p)
    if latent:
        rows, n_q, n_keys = N_LAT, SEQ, N_KEYS
        q_spec = pl.BlockSpec((SEQ, HEAD_DIM), lambda b, h: (b, h))
        kv_specs += [
            pl.BlockSpec((SEQ, HEAD_DIM), lambda b, h: (b, h)),
            pl.BlockSpec((SEQ, HEAD_DIM), lambda b, h: (b, v_col + h)),
        ]
        kv_args += (kd, p)
    else:
        rows, n_q, n_keys = N_CTX, CTX_LEN, CTX_LEN
        q_spec = pl.BlockSpec((CTX_LEN, HEAD_DIM), lambda b, h: (ctx_blk + b, h))
    return pl.pallas_call(
        functools.partial(_diff_attn_kernel, lam_init=lam_init, n_blocks=len(kv_args) // 2),
        out_shape=jax.ShapeDtypeStruct((rows, DIFF_WIDTH), BF16),
        grid=(BATCH, DIFF_HEADS),
        in_specs=[
            pl.BlockSpec((None, 4, DIFF_QK_DIM), lambda b, h: (layer_idx, 0, 0)),
            pl.BlockSpec((None, 1, HEAD_DIM), lambda b, h: (layer_idx, 0, 0)),
            q_spec, q_spec, *kv_specs,
        ],
        out_specs=pl.BlockSpec((n_q, HEAD_DIM), lambda b, h: (b, h)),
        scratch_shapes=_att_scratch(n_keys),
        compiler_params=_params(2),
        name="diff_attention_lat" if latent else "diff_attention_ctx",
    )(lam_vecs, subln.reshape(-1, 1, HEAD_DIM), qd0, qd1, *kv_args)


def _gqa_attn_kernel(q_ref, *refs, n_blocks):
    k_refs = refs[0:2 * n_blocks:2]
    v_refs = refs[1:2 * n_blocks:2]
    o_ref = refs[2 * n_blocks]
    k_all, v_aug, sa, ma, sb, mb = refs[2 * n_blocks + 1:]
    tq = ATT_ROWS // GQA_GROUP
    n_tiles = q_ref.shape[0] // tq
    _assemble_kv(k_refs, v_refs, k_all, v_aug)

    def load_q(t):
        rows = _tile_rows(t, tq)
        return jnp.concatenate([q_ref[rows, g * HEAD_DIM:(g + 1) * HEAD_DIM] for g in range(GQA_GROUP)], axis=0)

    def finish(t, r):
        rows = _tile_rows(t, tq)
        for g in range(GQA_GROUP):
            o_ref[rows, g * HEAD_DIM:(g + 1) * HEAD_DIM] = r[g * tq:(g + 1) * tq].astype(o_ref.dtype)

    _attn_pipeline(n_tiles, load_q, finish, k_all, v_aug, sa, ma, sb, mb)


def _gqa_attention(qg, kg, p, latent):
    v_col = COL_GV // HEAD_DIM
    ctx_blk = N_LAT // CTX_LEN
    gw = GQA_GROUP * HEAD_DIM
    kv_specs = [
        pl.BlockSpec((CTX_LEN, HEAD_DIM), lambda b, g: (ctx_blk + b, g)),
        pl.BlockSpec((CTX_LEN, HEAD_DIM), lambda b, g: (ctx_blk + b, v_col + g)),
    ]
    kv_args = (kg, p)
    if latent:
        rows, n_q, n_keys = N_LAT, SEQ, N_KEYS
        q_spec = pl.BlockSpec((SEQ, gw), lambda b, g: (b, g))
        kv_specs += [
            pl.BlockSpec((SEQ, HEAD_DIM), lambda b, g: (b, g)),
            pl.BlockSpec((SEQ, HEAD_DIM), lambda b, g: (b, v_col + g)),
        ]
        kv_args += (kg, p)
    else:
        rows, n_q, n_keys = N_CTX, CTX_LEN, CTX_LEN
        q_spec = pl.BlockSpec((CTX_LEN, gw), lambda b, g: (ctx_blk + b, g))
    return pl.pallas_call(
        functools.partial(_gqa_attn_kernel, n_blocks=len(kv_args) // 2),
        out_shape=jax.ShapeDtypeStruct((rows, GQA_WIDTH), BF16),
        grid=(BATCH, GQA_KV_HEADS),
        in_specs=[q_spec, *kv_specs],
        out_specs=pl.BlockSpec((n_q, gw), lambda b, g: (b, g)),
        scratch_shapes=_att_scratch(n_keys),
        compiler_params=_params(2),
        name="gqa_attention_lat" if latent else "gqa_attention_ctx",
    )(qg, *kv_args)


LRU_SLAB = 256
LRU_ROWS = 256


def _conv_centred(x, w, bias):
    n = x.shape[0]
    row = lax.broadcasted_iota(jnp.int32, x.shape, 0)
    xm1 = jnp.where(row >= 1, pltpu.roll(x, 1, 0), 0.0)
    xp1 = jnp.where(row < n - 1, pltpu.roll(x, n - 1, 0), 0.0)
    xp2 = jnp.where(row < n - 2, pltpu.roll(x, n - 2, 0), 0.0)
    return xm1 * w[0:1] + x * w[1:2] + xp1 * w[2:3] + xp2 * w[3:4] + bias


def _gelu_tanh(g):
    return 0.5 * g * (1.0 + jnp.tanh(0.7978845608028654 * (g + 0.044715 * (g * g * g))))


def _lru_kernel(*refs, need_ctx):
    if need_ctx:
        (xl_ref, xc_ref, gl_ref, gc_ref, cw_ref, cb_ref, wa_ref, ba_ref, wi_ref, bi_ref, al_ref,
         yl_ref, yc_ref, xconv_ref, a_ref, u_ref, hf_ref, hb_ref) = refs
    else:
        (xl_ref, xc_ref, gl_ref, cw_ref, cb_ref, wa_ref, ba_ref, wi_ref, bi_ref, al_ref,
         yl_ref, xconv_ref, a_ref, u_ref, hf_ref, hb_ref) = refs
    cw = cw_ref[...]
    cb = cb_ref[...]
    xconv_ref[0:CTX_LEN, :] = _conv_centred(xc_ref[...].astype(F32), cw, cb)
    xconv_ref[CTX_LEN:N_KEYS, :] = _conv_centred(xl_ref[...].astype(F32), cw, cb)

    z = -al_ref[...]
    sp = jnp.maximum(z, 0.0) + jnp.log1p(jnp.exp(-jnp.abs(z)))
    rate = (-LRU_C * LOG2E) * sp
    ba = ba_ref[...]
    bi = bi_ref[...]

    def gate_chunk(ci, carry):
        r0 = pl.multiple_of(ci * LRU_ROWS, LRU_ROWS)
        xc = xconv_ref[pl.ds(r0, LRU_ROWS), :]
        xb = xc.astype(BF16)
        for g in range(LRU_SLAB // LRU_BLOCK):
            sl = slice(g * LRU_BLOCK, (g + 1) * LRU_BLOCK)
            for d in range(2):
                ra = jnp.dot(xb[:, sl], wa_ref[d, g].astype(BF16), preferred_element_type=F32) + ba[d:d + 1, sl]
                ri = jnp.dot(xb[:, sl], wi_ref[d, g].astype(BF16), preferred_element_type=F32) + bi[d:d + 1, sl]
                a = jnp.exp2((0.5 * jnp.tanh(0.5 * ra) + 0.5) * rate[d:d + 1, sl])
                u = jnp.sqrt(1.0 - a * a) * ((0.5 * jnp.tanh(0.5 * ri) + 0.5) * xc[:, sl])
                a_ref[d, pl.ds(r0, LRU_ROWS), sl] = a
                u_ref[d, pl.ds(r0, LRU_ROWS), sl] = u
        return carry

    lax.fori_loop(0, N_KEYS // LRU_ROWS, gate_chunk, 0)

    row8 = lax.broadcasted_iota(jnp.int32, (SUBLANES, LRU_SLAB), 0)

    def block_scan(a, u, h_in, reverse):
        for k in (1, 2, 4):
            if reverse:
                keep = row8 < SUBLANES - k
                shift = SUBLANES - k
            else:
                keep = row8 >= k
                shift = k
            a_s = jnp.where(keep, pltpu.roll(a, shift, 0), 1.0)
            u_s = jnp.where(keep, pltpu.roll(u, shift, 0), 0.0)
            u = a * u_s + u
            a = a * a_s
        h = a * h_in + u
        last = h[0:1, :] if reverse else h[SUBLANES - 1:SUBLANES, :]
        return h, jnp.broadcast_to(last, h.shape)

    def scan_pair(base, n_blocks):
        def body(i, carry):
            hf, hb = carry
            rf = pl.multiple_of(base + i * SUBLANES, SUBLANES)
            rb = pl.multiple_of(base + (n_blocks - 1 - i) * SUBLANES, SUBLANES)
            h, hf = block_scan(a_ref[0, pl.ds(rf, SUBLANES), :], u_ref[0, pl.ds(rf, SUBLANES), :], hf, False)
            hf_ref[pl.ds(rf, SUBLANES), :] = h
            h, hb = block_scan(a_ref[1, pl.ds(rb, SUBLANES), :], u_ref[1, pl.ds(rb, SUBLANES), :], hb, True)
            hb_ref[pl.ds(rb, SUBLANES), :] = h
            return hf, hb
        return body

    zero = jnp.zeros((SUBLANES, LRU_SLAB), F32)
    carry = lax.fori_loop(0, CTX_LEN // SUBLANES, scan_pair(0, CTX_LEN // SUBLANES), (zero, zero), unroll=2)
    lax.fori_loop(0, SEQ // SUBLANES, scan_pair(CTX_LEN, SEQ // SUBLANES), carry, unroll=2)

    def out_chunk(ci, carry):
        r0 = pl.multiple_of(ci * LRU_ROWS, LRU_ROWS)
        h = hf_ref[pl.ds(CTX_LEN + r0, LRU_ROWS), :] + hb_ref[pl.ds(CTX_LEN + r0, LRU_ROWS), :]
        yl_ref[pl.ds(r0, LRU_ROWS), :] = (h * _gelu_tanh(gl_ref[pl.ds(r0, LRU_ROWS), :].astype(F32))).astype(BF16)
        return carry

    lax.fori_loop(0, SEQ // LRU_ROWS, out_chunk, 0)
    if need_ctx:
        h = hf_ref[0:CTX_LEN, :] + hb_ref[0:CTX_LEN, :]
        yc_ref[...] = (h * _gelu_tanh(gc_ref[...].astype(F32))).astype(BF16)


def _rglru(p, conv_w, conv_b, w_a, b_a, w_i, b_i, a_logit, layer, need_ctx):
    slab = LRU_SLAB
    n_slab = LRU_WIDTH // slab
    gpb = slab // LRU_BLOCK
    ctx_blk = N_LAT // CTX_LEN
    x_col = COL_RX // slab
    g_col = COL_RG // slab
    in_specs = [
        pl.BlockSpec((SEQ, slab), lambda b, s: (b, x_col + s)),
        pl.BlockSpec((CTX_LEN, slab), lambda b, s: (ctx_blk + b, x_col + s)),
        pl.BlockSpec((SEQ, slab), lambda b, s: (b, g_col + s)),
    ]
    args = [p, p, p]
    if need_ctx:
        in_specs.append(pl.BlockSpec((CTX_LEN, slab), lambda b, s: (ctx_blk + b, g_col + s)))
        args.append(p)
    in_specs += [
        pl.BlockSpec((None, CONV_WIDTH, slab), lambda b, s: (layer, 0, s)),
        pl.BlockSpec((None, 1, slab), lambda b, s: (layer, 0, s)),
        pl.BlockSpec((None, 2, gpb, LRU_BLOCK, LRU_BLOCK), lambda b, s: (layer, 0, s, 0, 0)),
        pl.BlockSpec((None, 2, slab), lambda b, s: (layer, 0, s)),
        pl.BlockSpec((None, 2, gpb, LRU_BLOCK, LRU_BLOCK), lambda b, s: (layer, 0, s, 0, 0)),
        pl.BlockSpec((None, 2, slab), lambda b, s: (layer, 0, s)),
        pl.BlockSpec((None, 2, slab), lambda b, s: (layer, 0, s)),
    ]
    args += [conv_w, conv_b.reshape(-1, 1, LRU_WIDTH), w_a, b_a, w_i, b_i, a_logit]
    out_shape = [jax.ShapeDtypeStruct((N_LAT, LRU_WIDTH), BF16)]
    out_specs = [pl.BlockSpec((SEQ, slab), lambda b, s: (b, s))]
    if need_ctx:
        out_shape.append(jax.ShapeDtypeStruct((N_CTX, LRU_WIDTH), BF16))
        out_specs.append(pl.BlockSpec((CTX_LEN, slab), lambda b, s: (b, s)))
    return pl.pallas_call(
        functools.partial(_lru_kernel, need_ctx=need_ctx),
        out_shape=out_shape,
        grid=(BATCH, n_slab),
        in_specs=in_specs,
        out_specs=out_specs,
        scratch_shapes=[
            pltpu.VMEM((N_KEYS, slab), F32),
            pltpu.VMEM((2, N_KEYS, slab), F32),
            pltpu.VMEM((2, N_KEYS, slab), F32),
            pltpu.VMEM((N_KEYS, slab), F32),
            pltpu.VMEM((N_KEYS, slab), F32),
        ],
        compiler_params=_params(2),
        name="rglru",
    )(*args)


def _outproj_kernel(d_ref, g_ref, l_ref, w_ref, *refs, n_first):
    *x_refs, gate_ref, o_ref, wb_ref = refs

    @pl.when(pl.program_id(1) == 0)
    def _():
        wb_ref[...] = w_ref[...].astype(BF16)

    acc = jnp.dot(d_ref[...], wb_ref[0:DIFF_WIDTH, :], preferred_element_type=F32)
    acc += jnp.dot(g_ref[...], wb_ref[DIFF_WIDTH:DIFF_WIDTH + GQA_WIDTH, :], preferred_element_type=F32)
    acc += jnp.dot(l_ref[...], wb_ref[DIFF_WIDTH + GQA_WIDTH:, :], preferred_element_type=F32)
    o_ref[...] = _tile_source(pl.program_id(1), n_first, x_refs) + gate_ref[...] * acc


def _out_projection(diff, gqa, lru, w_out, xs, mods, layer, rows):
    tm, tn = 1024, 512
    n_first = N_LAT // tm
    return pl.pallas_call(
        functools.partial(_outproj_kernel, n_first=n_first),
        out_shape=jax.ShapeDtypeStruct((rows, D_MODEL), F32),
        grid=(D_MODEL // tn, rows // tm),
        in_specs=[
            pl.BlockSpec((tm, DIFF_WIDTH), lambda j, i: (i, 0)),
            pl.BlockSpec((tm, GQA_WIDTH), lambda j, i: (i, 0)),
            pl.BlockSpec((tm, LRU_WIDTH), lambda j, i: (i, 0)),
            pl.BlockSpec((None, D_MODEL, tn), lambda j, i: (layer, 0, j)),
            *_split_specs((tm, tn), n_first, 1, 0, len(xs)),
            _mod_spec(layer, 2, tm, tn, 1, 0),
        ],
        out_specs=pl.BlockSpec((tm, tn), lambda j, i: (i, j)),
        scratch_shapes=[pltpu.VMEM((D_MODEL, tn), BF16)],
        compiler_params=_params(2),
        name="out_projection",
    )(diff, gqa, lru, w_out, *xs, mods)


MLP_TM = 512
MLP_TF = 512
MLP_TN = 1024
MLP_ROWS = 64


def _mlp_kernel(x_ref, sh_ref, sc_ref, gate_ref, w1_ref, w2_ref, fin_ref, o_ref, h_ref, *, final):
    f = pl.program_id(1)

    def row_chunks(body):
        def step(r, carry):
            body(pl.ds(pl.multiple_of(r * MLP_ROWS, MLP_ROWS), MLP_ROWS))
            return carry
        lax.fori_loop(0, x_ref.shape[0] // MLP_ROWS, step, 0)

    @pl.when(f == 0)
    def _():
        def body(rows):
            y = _rms(x_ref[rows, :])
            h_ref[rows, :] = (y * (1.0 + sc_ref[...]) + sh_ref[...]).astype(BF16)
            o_ref[rows, :] = jnp.zeros((MLP_ROWS, D_MODEL), F32)
        row_chunks(body)

    t = jnp.dot(h_ref[...], w1_ref[...], preferred_element_type=F32)
    t = jnp.maximum(t, 0.0)
    tb = (t * t).astype(BF16)

    for c in range(D_MODEL // MLP_TN):
        sl = slice(c * MLP_TN, (c + 1) * MLP_TN)
        o_ref[:, sl] += jnp.dot(tb, w2_ref[:, sl], preferred_element_type=F32)

    @pl.when(f == pl.num_programs(1) - 1)
    def _():
        def body(rows):
            y = x_ref[rows, :] + gate_ref[...] * o_ref[rows, :]
            if final:
                y = _rms(y) * fin_ref[...]
            o_ref[rows, :] = y
        row_chunks(body)


def _mlp(x, mods, w1t, w2b, final_norm, layer, rows, final):
    tm, tf = MLP_TM, MLP_TF
    return pl.pallas_call(
        functools.partial(_mlp_kernel, final=final),
        out_shape=jax.ShapeDtypeStruct((rows, D_MODEL), F32),
        grid=(rows // tm, D_FF // tf),
        in_specs=[
            pl.BlockSpec((tm, D_MODEL), lambda i, f: (i, 0)),
            _mod_spec(layer, 3, tm, D_MODEL, 0, None),
            _mod_spec(layer, 4, tm, D_MODEL, 0, None),
            _mod_spec(layer, 5, tm, D_MODEL, 0, None),
            pl.BlockSpec((None, D_MODEL, tf), lambda i, f: (f, 0, 0)),
            pl.BlockSpec((tf, D_MODEL), lambda i, f: (f, 0)),
            pl.BlockSpec((1, D_MODEL), lambda i, f: (0, 0)),
        ],
        out_specs=pl.BlockSpec((tm, D_MODEL), lambda i, f: (i, 0)),
        scratch_shapes=[pltpu.VMEM((tm, D_MODEL), BF16)],
        compiler_params=_params(2),
        name="mlp_final" if final else "mlp",
    )(x, mods, mods, mods, w1t, w2b, final_norm.reshape(1, D_MODEL))


def kernel(x, c, ctx, c_ctx, w_mod, b_mod, w_in, diff_lambda, diff_subln, gqa_q_norm, gqa_k_norm,
           lru_conv_w, lru_conv_b, lru_w_a, lru_b_a, lru_w_i, lru_b_i, lru_a_logit,
           w_out, w_ff1, w_ff2, final_norm):
    depth = w_mod.shape[0]
    c8 = jnp.concatenate([c, c_ctx[None], jnp.zeros((8 - BATCH - 1, D_MODEL), F32)], axis=0)
    mods = _modulation(c8, w_mod, b_mod).reshape(depth * 8 * N_MOD, 1, D_MODEL)
    tables = _rope_tables(DIFF_QK_DIM) + _rope_tables(HEAD_DIM)

    xs = (x.reshape(N_LAT, D_MODEL), ctx.reshape(N_CTX, D_MODEL))
    for l in range(depth):
        need_ctx = l < depth - 1
        rows = N_TOK if need_ctx else N_LAT
        h = _norm_mod(xs, mods, l)
        p, w1t, w2b = _in_projection(h, w_in, w_ff1, w_ff2, l)
        qd0, qd1, kd, qg, kg = _qk_prep(p, gqa_q_norm, gqa_k_norm, tables, l)
        diff = _diff_attention(qd0, qd1, kd, p, diff_lambda, diff_subln, l, True)
        gqa = _gqa_attention(qg, kg, p, True)
        lru = _rglru(p, lru_conv_w, lru_conv_b, lru_w_a, lru_b_a, lru_w_i, lru_b_i, lru_a_logit, l, need_ctx)
        if need_ctx:
            diff = jnp.concatenate([diff, _diff_attention(qd0, qd1, kd, p, diff_lambda, diff_subln, l, False)], 0)
            gqa = jnp.concatenate([gqa, _gqa_attention(qg, kg, p, False)], 0)
            lru = jnp.concatenate(lru, 0)
        else:
            lru = lru[0]
        xt = _out_projection(diff, gqa, lru, w_out, xs, mods, l, rows)
        xs = (_mlp(xt, mods, w1t, w2b, final_norm, l, rows, not need_ctx),)
    return xs[0].reshape(BATCH, SEQ, D_MODEL)
```

```python
import functools
import math

import jax
import jax.numpy as jnp
from jax import lax
from jax.experimental import pallas as pl
from jax.experimental.pallas import tpu as pltpu

D_MODEL = 4096
BATCH = 4
SEQ = 2048
DEPTH = 2
GRID_W = 64
CTX_LEN = 256
HEAD_DIM = 128
DIFF_WIDTH = D_MODEL // 4
DIFF_HEADS = DIFF_WIDTH // HEAD_DIM
DIFF_QK_DIM = HEAD_DIM // 2
GQA_WIDTH = D_MODEL // 4
GQA_HEADS = GQA_WIDTH // HEAD_DIM
GQA_KV_HEADS = 2
GQA_GROUP = GQA_HEADS // GQA_KV_HEADS
LRU_WIDTH = D_MODEL // 2
LRU_BLOCK = 128
LRU_BLOCKS = LRU_WIDTH // LRU_BLOCK
CONV_WIDTH = 4
LRU_C = 8.0
D_FF = 4 * D_MODEL
IN_COLS = 3 * DIFF_WIDTH + GQA_WIDTH + 2 * GQA_KV_HEADS * HEAD_DIM + 2 * LRU_WIDTH
ROPE_THETA = 10000.0
EPS = 1e-6
N_MOD = 6

N_LAT = BATCH * SEQ
N_CTX = BATCH * CTX_LEN
N_TOK = N_LAT + N_CTX
N_KEYS = CTX_LEN + SEQ

COL_DQ = 0
COL_DK = COL_DQ + DIFF_WIDTH
COL_DV = COL_DK + DIFF_WIDTH
COL_GQ = COL_DV + DIFF_WIDTH
COL_GK = COL_GQ + GQA_WIDTH
COL_GV = COL_GK + GQA_KV_HEADS * HEAD_DIM
COL_RX = COL_GV + GQA_KV_HEADS * HEAD_DIM
COL_RG = COL_RX + LRU_WIDTH

LANES = 128
SUBLANES = 8
VMEM_LIMIT = 56 * 1024 * 1024

F32 = jnp.float32
BF16 = jnp.bfloat16


def _params(n_axes, vmem=VMEM_LIMIT):
    return pltpu.CompilerParams(dimension_semantics=("arbitrary",) * n_axes, vmem_limit_bytes=vmem)


def _rms(x):
    return x * lax.rsqrt(jnp.mean(x * x, axis=-1, keepdims=True) + EPS)


def _mod_row(tile, tm):
    return jnp.minimum((tile * tm) // SEQ, BATCH)


def _mod_kernel(c_ref, w_ref, b_ref, o_ref):
    c = c_ref[...]
    s = (c * jax.nn.sigmoid(c)).astype(BF16)
    o_ref[...] = jnp.dot(s, w_ref[...].astype(BF16), preferred_element_type=F32) + b_ref[...]


def _modulation(c8, w_mod, b_mod):
    depth = w_mod.shape[0]
    tn = 512
    return pl.pallas_call(
        _mod_kernel,
        out_shape=jax.ShapeDtypeStruct((depth, 8, N_MOD * D_MODEL), F32),
        grid=(depth, N_MOD * D_MODEL // tn),
        in_specs=[
            pl.BlockSpec((8, D_MODEL), lambda l, j: (0, 0)),
            pl.BlockSpec((None, D_MODEL, tn), lambda l, j: (l, 0, j)),
            pl.BlockSpec((None, 1, tn), lambda l, j: (l, 0, j)),
        ],
        out_specs=pl.BlockSpec((None, 8, tn), lambda l, j: (l, 0, j)),
        compiler_params=_params(2),
        name="modulation",
    )(c8, w_mod, b_mod.reshape(depth, 1, N_MOD * D_MODEL))


def _mod_spec(layer, which, tm, tn, row_axis, col_axis):
    def index_map(*ids):
        row = _mod_row(ids[row_axis], tm)
        col = 0 if col_axis is None else ids[col_axis]
        return (layer * 8 * N_MOD + row * N_MOD + which, 0, col)
    return pl.BlockSpec((None, 1, tn), index_map)


def _tile_source(i, n_first, refs):
    if len(refs) == 1:
        return refs[0][...]
    return jnp.where(i < n_first, refs[0][...], refs[1][...])


def _split_specs(block, n_first, tile_axis, col_axis, n_sources):
    def col(ids):
        return 0 if col_axis is None else ids[col_axis]
    if n_sources == 1:
        return [pl.BlockSpec(block, lambda *ids: (ids[tile_axis], col(ids)))]
    return [
        pl.BlockSpec(block, lambda *ids: (jnp.minimum(ids[tile_axis], n_first - 1), col(ids))),
        pl.BlockSpec(block, lambda *ids: (jnp.maximum(ids[tile_axis] - n_first, 0), col(ids))),
    ]


def _norm_mod_kernel(*refs, n_first):
    *x_refs, sh_ref, sc_ref, o_ref = refs
    y = _rms(_tile_source(pl.program_id(0), n_first, x_refs))
    o_ref[...] = (y * (1.0 + sc_ref[...]) + sh_ref[...]).astype(BF16)


def _norm_mod(xs, mods, layer):
    tm = 256
    n_first = N_LAT // tm
    return pl.pallas_call(
        functools.partial(_norm_mod_kernel, n_first=n_first),
        out_shape=jax.ShapeDtypeStruct((N_TOK, D_MODEL), BF16),
        grid=(N_TOK // tm,),
        in_specs=[
            *_split_specs((tm, D_MODEL), n_first, 0, None, len(xs)),
            _mod_spec(layer, 0, tm, D_MODEL, 0, None),
            _mod_spec(layer, 1, tm, D_MODEL, 0, None),
        ],
        out_specs=pl.BlockSpec((tm, D_MODEL), lambda i: (i, 0)),
        compiler_params=_params(1),
        name="norm_mod",
    )(*xs, mods, mods)


def _inproj_kernel(h_ref, w_ref, o_ref, wb_ref):
    @pl.when(pl.program_id(1) == 0)
    def _():
        wb_ref[...] = w_ref[...].astype(BF16)

    o_ref[...] = jnp.dot(h_ref[...], wb_ref[...], preferred_element_type=F32).astype(o_ref.dtype)


def _in_projection(h, w_in, layer):
    tm, tn = 1024, 512
    return pl.pallas_call(
        _inproj_kernel,
        out_shape=jax.ShapeDtypeStruct((N_TOK, IN_COLS), BF16),
        grid=(IN_COLS // tn, N_TOK // tm),
        in_specs=[
            pl.BlockSpec((tm, D_MODEL), lambda j, i: (i, 0)),
            pl.BlockSpec((None, D_MODEL, tn), lambda j, i: (layer, 0, j)),
        ],
        out_specs=pl.BlockSpec((tm, tn), lambda j, i: (i, j)),
        scratch_shapes=[pltpu.VMEM((D_MODEL, tn), BF16)],
        compiler_params=_params(2),
        name="in_projection",
    )(h, w_in)


def _rope_tables(dim):
    n_freq = dim // 4
    pos = jnp.arange(SEQ)
    row = (pos // GRID_W).astype(F32)
    col = (pos % GRID_W).astype(F32)
    inv = ROPE_THETA ** (-jnp.arange(n_freq, dtype=F32) / n_freq)
    ang_r = row[:, None] * inv
    ang_c = col[:, None] * inv
    cos = jnp.concatenate([jnp.cos(ang_r), jnp.cos(ang_r), jnp.cos(ang_c), jnp.cos(ang_c)], axis=1)
    sin = jnp.concatenate([-jnp.sin(ang_r), jnp.sin(ang_r), -jnp.sin(ang_c), jnp.sin(ang_c)], axis=1)
    reps = LANES // dim
    cos = jnp.tile(cos, (1, reps))
    sin = jnp.tile(sin, (1, reps))
    cos = jnp.concatenate([cos, jnp.ones((CTX_LEN, LANES), F32)], axis=0)
    sin = jnp.concatenate([sin, jnp.zeros((CTX_LEN, LANES), F32)], axis=0)
    return cos, sin


def _rope(x, cos, sin, first, half):
    partner = jnp.where(first, pltpu.roll(x, LANES - half, 1), pltpu.roll(x, half, 1))
    return x * cos + partner * sin


def _qkprep_kernel(dq_ref, dk_ref, gq_ref, gk_ref, cd_ref, sd_ref, cg_ref, sg_ref, qgain_ref, kgain_ref,
                   qd0_ref, qd1_ref, kd_ref, qg_ref, kg_ref):
    cd, sd, cg, sg = cd_ref[...], sd_ref[...], cg_ref[...], sg_ref[...]
    qgain, kgain = qgain_ref[...], kgain_ref[...]
    lane = lax.broadcasted_iota(jnp.int32, cd.shape, 1)
    first_d = (lane & (DIFF_QK_DIM // 2 - 1)) < DIFF_QK_DIM // 4
    first_g = (lane & (HEAD_DIM // 2 - 1)) < HEAD_DIM // 4
    map0 = lane < DIFF_QK_DIM
    d_scale = DIFF_QK_DIM ** -0.5 * LOG2E
    g_scale = HEAD_DIM ** -0.5 * LOG2E
    for c in range(DIFF_WIDTH // LANES):
        sl = slice(c * LANES, (c + 1) * LANES)
        q = _rope(dq_ref[:, sl].astype(F32), cd, sd, first_d, DIFF_QK_DIM // 4) * d_scale
        qd0_ref[:, sl] = jnp.where(map0, q, 0.0).astype(BF16)
        qd1_ref[:, sl] = jnp.where(map0, 0.0, q).astype(BF16)
        kd_ref[:, sl] = _rope(dk_ref[:, sl].astype(F32), cd, sd, first_d, DIFF_QK_DIM // 4).astype(BF16)
    for c in range(GQA_WIDTH // LANES):
        sl = slice(c * LANES, (c + 1) * LANES)
        qn = _rms(gq_ref[:, sl].astype(F32)) * qgain
        qg_ref[:, sl] = (_rope(qn, cg, sg, first_g, HEAD_DIM // 4) * g_scale).astype(BF16)
    for c in range(GQA_KV_HEADS):
        sl = slice(c * LANES, (c + 1) * LANES)
        kn = _rms(gk_ref[:, sl].astype(F32)) * kgain
        kg_ref[:, sl] = _rope(kn, cg, sg, first_g, HEAD_DIM // 4).astype(BF16)


def _qk_prep(p, q_gain, k_gain, tables, layer):
    tm = 256
    cos_d, sin_d, cos_g, sin_g = tables
    lat_tiles = N_LAT // tm

    def pos_block(i):
        return jnp.where(i < lat_tiles, i % (SEQ // tm), SEQ // tm)

    kv_w = GQA_KV_HEADS * HEAD_DIM
    tab_spec = pl.BlockSpec((tm, LANES), lambda i: (pos_block(i), 0))
    gain_spec = pl.BlockSpec((None, 1, HEAD_DIM), lambda i: (layer, 0, 0))
    wide = lambda col: pl.BlockSpec((tm, DIFF_WIDTH), lambda i: (i, col // DIFF_WIDTH))
    return pl.pallas_call(
        _qkprep_kernel,
        out_shape=[
            jax.ShapeDtypeStruct((N_TOK, DIFF_WIDTH), BF16),
            jax.ShapeDtypeStruct((N_TOK, DIFF_WIDTH), BF16),
            jax.ShapeDtypeStruct((N_TOK, DIFF_WIDTH), BF16),
            jax.ShapeDtypeStruct((N_TOK, GQA_WIDTH), BF16),
            jax.ShapeDtypeStruct((N_TOK, kv_w), BF16),
        ],
        grid=(N_TOK // tm,),
        in_specs=[
            wide(COL_DQ), wide(COL_DK), wide(COL_GQ),
            pl.BlockSpec((tm, kv_w), lambda i: (i, COL_GK // kv_w)),
            tab_spec, tab_spec, tab_spec, tab_spec, gain_spec, gain_spec,
        ],
        out_specs=[
            pl.BlockSpec((tm, DIFF_WIDTH), lambda i: (i, 0)),
            pl.BlockSpec((tm, DIFF_WIDTH), lambda i: (i, 0)),
            pl.BlockSpec((tm, DIFF_WIDTH), lambda i: (i, 0)),
            pl.BlockSpec((tm, GQA_WIDTH), lambda i: (i, 0)),
            pl.BlockSpec((tm, kv_w), lambda i: (i, 0)),
        ],
        compiler_params=_params(1),
        name="qk_prep",
    )(p, p, p, p, cos_d, sin_d, cos_g, sin_g, q_gain.reshape(-1, 1, HEAD_DIM), k_gain.reshape(-1, 1, HEAD_DIM))


ATT_ROWS = 512
ATT_KCHUNK = 768
LOG2E = 1.4426950408889634


def _assemble_kv(k_refs, v_refs, k_all, v_aug):
    r = 0
    for k, v in zip(k_refs, v_refs):
        n = k.shape[0]
        k_all[r:r + n, :] = k[...]
        v_aug[r:r + n, 0:HEAD_DIM] = v[...]
        r += n
    v_aug[:, HEAD_DIM:] = jnp.ones((v_aug.shape[0], HEAD_DIM), BF16)


def _tile_scores(qq, k_all, s_ref, m_ref):
    s = lax.dot_general(qq, k_all[...], (((1,), (1,)), ((), ())), preferred_element_type=F32)
    s_ref[...] = s
    m_ref[...] = jnp.max(s, axis=-1, keepdims=True)


def _tile_pv(s_ref, m_ref, v_aug):
    m = m_ref[...]
    n_keys = s_ref.shape[1]
    chunk = min(ATT_KCHUNK, n_keys)
    acc = None
    for c in range(0, n_keys, chunk):
        e = jnp.exp2(s_ref[:, c:c + chunk] - m).astype(BF16)
        o = jnp.dot(e, v_aug[c:c + chunk, :], preferred_element_type=F32)
        acc = o if acc is None else acc + o
    return acc[:, :HEAD_DIM] / acc[:, HEAD_DIM:]


def _attn_pipeline(n_tiles, load_q, finish, k_all, v_aug, sa, ma, sb, mb):
    _tile_scores(load_q(0), k_all, sa, ma)
    if n_tiles == 1:
        finish(0, _tile_pv(sa, ma, v_aug))
        return
    assert n_tiles % 2 == 0

    def pair(j, carry):
        t0 = 2 * j
        r0 = _tile_pv(sa, ma, v_aug)
        _tile_scores(load_q(t0 + 1), k_all, sb, mb)
        finish(t0, r0)
        r1 = _tile_pv(sb, mb, v_aug)
        _tile_scores(load_q(jnp.minimum(t0 + 2, n_tiles - 1)), k_all, sa, ma)
        finish(t0 + 1, r1)
        return carry

    lax.fori_loop(0, n_tiles // 2, pair, 0)


def _att_scratch(n_keys):
    return [
        pltpu.VMEM((n_keys, HEAD_DIM), BF16),
        pltpu.VMEM((n_keys, 2 * HEAD_DIM), BF16),
        pltpu.VMEM((ATT_ROWS, n_keys), F32),
        pltpu.VMEM((ATT_ROWS, 1), F32),
        pltpu.VMEM((ATT_ROWS, n_keys), F32),
        pltpu.VMEM((ATT_ROWS, 1), F32),
    ]


def _tile_rows(t, tq):
    return pl.ds(t * tq if isinstance(t, int) else pl.multiple_of(t * tq, tq), tq)


def _diff_attn_kernel(lam_ref, gain_ref, q0_ref, q1_ref, *refs, lam_init, n_blocks, cast):
    k_refs = refs[0:2 * n_blocks:2]
    v_refs = refs[1:2 * n_blocks:2]
    refs = refs[2 * n_blocks:]
    if cast:
        f1_ref, o_ref, c1_ref = refs[:3]
        refs = refs[3:]
        for c in range(D_FF // MLP_TF):
            c1_ref[c] = f1_ref[:, c * MLP_TF:(c + 1) * MLP_TF].astype(BF16)
    else:
        o_ref = refs[0]
        refs = refs[1:]
    k_all, v_aug, sa, ma, sb, mb = refs
    tq = ATT_ROWS // 2
    n_tiles = q0_ref.shape[0] // tq
    lv = lam_ref[...]
    lam = (jnp.exp(jnp.sum(lv[0:1] * lv[1:2], axis=-1, keepdims=True))
           - jnp.exp(jnp.sum(lv[2:3] * lv[3:4], axis=-1, keepdims=True)) + lam_init)
    gain = gain_ref[...] * (1.0 - lam_init)
    _assemble_kv(k_refs, v_refs, k_all, v_aug)

    def load_q(t):
        return jnp.concatenate([q0_ref[_tile_rows(t, tq), :], q1_ref[_tile_rows(t, tq), :]], axis=0)

    def finish(t, r):
        o = r[:tq] - lam * r[tq:]
        o_ref[_tile_rows(t, tq), :] = (_rms(o) * gain).astype(o_ref.dtype)

    _attn_pipeline(n_tiles, load_q, finish, k_all, v_aug, sa, ma, sb, mb)


def _diff_attention(qd0, qd1, kd, p, lam_vecs, subln, layer_idx, latent, w_ff1=None):
    lam_init = 0.8 - 0.6 * math.exp(-0.3 * layer_idx)
    cast = w_ff1 is not None
    v_col = COL_DV // HEAD_DIM
    ctx_blk = N_LAT // CTX_LEN
    kv_specs = [
        pl.BlockSpec((CTX_LEN, HEAD_DIM), lambda b, h: (ctx_blk + b, h)),
        pl.BlockSpec((CTX_LEN, HEAD_DIM), lambda b, h: (ctx_blk + b, v_col + h)),
    ]
    kv_args = (kd, p)
    if latent:
        rows, n_q, n_keys = N_LAT, SEQ, N_KEYS
        q_spec = pl.BlockSpec((SEQ, HEAD_DIM), lambda b, h: (b, h))
        kv_specs += [
            pl.BlockSpec((SEQ, HEAD_DIM), lambda b, h: (b, h)),
            pl.BlockSpec((SEQ, HEAD_DIM), lambda b, h: (b, v_col + h)),
        ]
        kv_args += (kd, p)
    else:
        rows, n_q, n_keys = N_CTX, CTX_LEN, CTX_LEN
        q_spec = pl.BlockSpec((CTX_LEN, HEAD_DIM), lambda b, h: (ctx_blk + b, h))
    out_shape = [jax.ShapeDtypeStruct((rows, DIFF_WIDTH), BF16)]
    out_specs = [pl.BlockSpec((n_q, HEAD_DIM), lambda b, h: (b, h))]
    if cast:
        r1 = D_MODEL // (BATCH * DIFF_HEADS)
        kv_specs.append(pl.BlockSpec((None, r1, D_FF), lambda b, h: (layer_idx, b * DIFF_HEADS + h, 0)))
        kv_args += (w_ff1,)
        out_shape.append(jax.ShapeDtypeStruct((D_FF // MLP_TF, D_MODEL, MLP_TF), BF16))
        out_specs.append(pl.BlockSpec((D_FF // MLP_TF, r1, MLP_TF), lambda b, h: (0, b * DIFF_HEADS + h, 0)))
    out = pl.pallas_call(
        functools.partial(_diff_attn_kernel, lam_init=lam_init, n_blocks=(len(kv_args) - cast) // 2, cast=cast),
        out_shape=out_shape,
        grid=(BATCH, DIFF_HEADS),
        in_specs=[
            pl.BlockSpec((None, 4, DIFF_QK_DIM), lambda b, h: (layer_idx, 0, 0)),
            pl.BlockSpec((None, 1, HEAD_DIM), lambda b, h: (layer_idx, 0, 0)),
            q_spec, q_spec, *kv_specs,
        ],
        out_specs=out_specs,
        scratch_shapes=_att_scratch(n_keys),
        compiler_params=_params(2),
        name="diff_attention_lat" if latent else "diff_attention_ctx",
    )(lam_vecs, subln.reshape(-1, 1, HEAD_DIM), qd0, qd1, *kv_args)
    return tuple(out) if cast else out[0]


def _gqa_attn_kernel(q_ref, *refs, n_blocks):
    k_refs = refs[0:2 * n_blocks:2]
    v_refs = refs[1:2 * n_blocks:2]
    o_ref = refs[2 * n_blocks]
    k_all, v_aug, sa, ma, sb, mb = refs[2 * n_blocks + 1:]
    tq = ATT_ROWS // GQA_GROUP
    n_tiles = q_ref.shape[0] // tq
    _assemble_kv(k_refs, v_refs, k_all, v_aug)

    def load_q(t):
        rows = _tile_rows(t, tq)
        return jnp.concatenate([q_ref[rows, g * HEAD_DIM:(g + 1) * HEAD_DIM] for g in range(GQA_GROUP)], axis=0)

    def finish(t, r):
        rows = _tile_rows(t, tq)
        for g in range(GQA_GROUP):
            o_ref[rows, g * HEAD_DIM:(g + 1) * HEAD_DIM] = r[g * tq:(g + 1) * tq].astype(o_ref.dtype)

    _attn_pipeline(n_tiles, load_q, finish, k_all, v_aug, sa, ma, sb, mb)


def _gqa_attention(qg, kg, p, latent):
    v_col = COL_GV // HEAD_DIM
    ctx_blk = N_LAT // CTX_LEN
    gw = GQA_GROUP * HEAD_DIM
    kv_specs = [
        pl.BlockSpec((CTX_LEN, HEAD_DIM), lambda b, g: (ctx_blk + b, g)),
        pl.BlockSpec((CTX_LEN, HEAD_DIM), lambda b, g: (ctx_blk + b, v_col + g)),
    ]
    kv_args = (kg, p)
    if latent:
        rows, n_q, n_keys = N_LAT, SEQ, N_KEYS
        q_spec = pl.BlockSpec((SEQ, gw), lambda b, g: (b, g))
        kv_specs += [
            pl.BlockSpec((SEQ, HEAD_DIM), lambda b, g: (b, g)),
            pl.BlockSpec((SEQ, HEAD_DIM), lambda b, g: (b, v_col + g)),
        ]
        kv_args += (kg, p)
    else:
        rows, n_q, n_keys = N_CTX, CTX_LEN, CTX_LEN
        q_spec = pl.BlockSpec((CTX_LEN, gw), lambda b, g: (ctx_blk + b, g))
    return pl.pallas_call(
        functools.partial(_gqa_attn_kernel, n_blocks=len(kv_args) // 2),
        out_shape=jax.ShapeDtypeStruct((rows, GQA_WIDTH), BF16),
        grid=(BATCH, GQA_KV_HEADS),
        in_specs=[q_spec, *kv_specs],
        out_specs=pl.BlockSpec((n_q, gw), lambda b, g: (b, g)),
        scratch_shapes=_att_scratch(n_keys),
        compiler_params=_params(2),
        name="gqa_attention_lat" if latent else "gqa_attention_ctx",
    )(qg, *kv_args)


LRU_SLAB = 256
LRU_ROWS = 256


def _conv_centred(x, w, bias):
    n = x.shape[0]
    row = lax.broadcasted_iota(jnp.int32, x.shape, 0)
    xm1 = jnp.where(row >= 1, pltpu.roll(x, 1, 0), 0.0)
    xp1 = jnp.where(row < n - 1, pltpu.roll(x, n - 1, 0), 0.0)
    xp2 = jnp.where(row < n - 2, pltpu.roll(x, n - 2, 0), 0.0)
    return xm1 * w[0:1] + x * w[1:2] + xp1 * w[2:3] + xp2 * w[3:4] + bias


def _gelu_tanh(g):
    return 0.5 * g * (1.0 + jnp.tanh(0.7978845608028654 * (g + 0.044715 * (g * g * g))))


def _lru_kernel(*refs, need_ctx):
    if need_ctx:
        (xl_ref, xc_ref, gl_ref, gc_ref, cw_ref, cb_ref, wa_ref, ba_ref, wi_ref, bi_ref, al_ref, f2_ref,
         yl_ref, yc_ref, c2_ref, xconv_ref, a_ref, u_ref, hf_ref, hb_ref) = refs
    else:
        (xl_ref, xc_ref, gl_ref, cw_ref, cb_ref, wa_ref, ba_ref, wi_ref, bi_ref, al_ref, f2_ref,
         yl_ref, c2_ref, xconv_ref, a_ref, u_ref, hf_ref, hb_ref) = refs

    def cast_rows(r, carry):
        rows = pl.ds(pl.multiple_of(r * LRU_ROWS, LRU_ROWS), LRU_ROWS)
        c2_ref[rows, :] = f2_ref[rows, :].astype(BF16)
        return carry

    lax.fori_loop(0, f2_ref.shape[0] // LRU_ROWS, cast_rows, 0)

    cw = cw_ref[...]
    cb = cb_ref[...]
    xconv_ref[0:CTX_LEN, :] = _conv_centred(xc_ref[...].astype(F32), cw, cb)
    xconv_ref[CTX_LEN:N_KEYS, :] = _conv_centred(xl_ref[...].astype(F32), cw, cb)

    z = -al_ref[...]
    sp = jnp.maximum(z, 0.0) + jnp.log1p(jnp.exp(-jnp.abs(z)))
    rate = (-LRU_C * LOG2E) * sp
    ba = ba_ref[...]
    bi = bi_ref[...]

    def gate_chunk(ci, carry):
        r0 = pl.multiple_of(ci * LRU_ROWS, LRU_ROWS)
        xc = xconv_ref[pl.ds(r0, LRU_ROWS), :]
        xb = xc.astype(BF16)
        for g in range(LRU_SLAB // LRU_BLOCK):
            sl = slice(g * LRU_BLOCK, (g + 1) * LRU_BLOCK)
            for d in range(2):
                ra = jnp.dot(xb[:, sl], wa_ref[d, g].astype(BF16), preferred_element_type=F32) + ba[d:d + 1, sl]
                ri = jnp.dot(xb[:, sl], wi_ref[d, g].astype(BF16), preferred_element_type=F32) + bi[d:d + 1, sl]
                a = jnp.exp2((0.5 * jnp.tanh(0.5 * ra) + 0.5) * rate[d:d + 1, sl])
                u = jnp.sqrt(1.0 - a * a) * ((0.5 * jnp.tanh(0.5 * ri) + 0.5) * xc[:, sl])
                a_ref[d, pl.ds(r0, LRU_ROWS), sl] = a
                u_ref[d, pl.ds(r0, LRU_ROWS), sl] = u
        return carry

    lax.fori_loop(0, N_KEYS // LRU_ROWS, gate_chunk, 0)

    row8 = lax.broadcasted_iota(jnp.int32, (SUBLANES, LRU_SLAB), 0)

    def block_scan(a, u, h_in, reverse):
        for k in (1, 2, 4):
            if reverse:
                keep = row8 < SUBLANES - k
                shift = SUBLANES - k
            else:
                keep = row8 >= k
                shift = k
            a_s = jnp.where(keep, pltpu.roll(a, shift, 0), 1.0)
            u_s = jnp.where(keep, pltpu.roll(u, shift, 0), 0.0)
            u = a * u_s + u
            a = a * a_s
        h = a * h_in + u
        last = h[0:1, :] if reverse else h[SUBLANES - 1:SUBLANES, :]
        return h, jnp.broadcast_to(last, h.shape)

    def scan_pair(base, n_blocks):
        def body(i, carry):
            hf, hb = carry
            rf = pl.multiple_of(base + i * SUBLANES, SUBLANES)
            rb = pl.multiple_of(base + (n_blocks - 1 - i) * SUBLANES, SUBLANES)
            h, hf = block_scan(a_ref[0, pl.ds(rf, SUBLANES), :], u_ref[0, pl.ds(rf, SUBLANES), :], hf, False)
            hf_ref[pl.ds(rf, SUBLANES), :] = h
            h, hb = block_scan(a_ref[1, pl.ds(rb, SUBLANES), :], u_ref[1, pl.ds(rb, SUBLANES), :], hb, True)
            hb_ref[pl.ds(rb, SUBLANES), :] = h
            return hf, hb
        return body

    zero = jnp.zeros((SUBLANES, LRU_SLAB), F32)
    carry = lax.fori_loop(0, CTX_LEN // SUBLANES, scan_pair(0, CTX_LEN // SUBLANES), (zero, zero), unroll=2)
    lax.fori_loop(0, SEQ // SUBLANES, scan_pair(CTX_LEN, SEQ // SUBLANES), carry, unroll=2)

    def out_chunk(ci, carry):
        r0 = pl.multiple_of(ci * LRU_ROWS, LRU_ROWS)
        h = hf_ref[pl.ds(CTX_LEN + r0, LRU_ROWS), :] + hb_ref[pl.ds(CTX_LEN + r0, LRU_ROWS), :]
        yl_ref[pl.ds(r0, LRU_ROWS), :] = (h * _gelu_tanh(gl_ref[pl.ds(r0, LRU_ROWS), :].astype(F32))).astype(BF16)
        return carry

    lax.fori_loop(0, SEQ // LRU_ROWS, out_chunk, 0)
    if need_ctx:
        h = hf_ref[0:CTX_LEN, :] + hb_ref[0:CTX_LEN, :]
        yc_ref[...] = (h * _gelu_tanh(gc_ref[...].astype(F32))).astype(BF16)


def _rglru(p, conv_w, conv_b, w_a, b_a, w_i, b_i, a_logit, w_ff2, layer, need_ctx):
    slab = LRU_SLAB
    n_slab = LRU_WIDTH // slab
    gpb = slab // LRU_BLOCK
    ctx_blk = N_LAT // CTX_LEN
    x_col = COL_RX // slab
    g_col = COL_RG // slab
    in_specs = [
        pl.BlockSpec((SEQ, slab), lambda b, s: (b, x_col + s)),
        pl.BlockSpec((CTX_LEN, slab), lambda b, s: (ctx_blk + b, x_col + s)),
        pl.BlockSpec((SEQ, slab), lambda b, s: (b, g_col + s)),
    ]
    args = [p, p, p]
    if need_ctx:
        in_specs.append(pl.BlockSpec((CTX_LEN, slab), lambda b, s: (ctx_blk + b, g_col + s)))
        args.append(p)
    in_specs += [
        pl.BlockSpec((None, CONV_WIDTH, slab), lambda b, s: (layer, 0, s)),
        pl.BlockSpec((None, 1, slab), lambda b, s: (layer, 0, s)),
        pl.BlockSpec((None, 2, gpb, LRU_BLOCK, LRU_BLOCK), lambda b, s: (layer, 0, s, 0, 0)),
        pl.BlockSpec((None, 2, slab), lambda b, s: (layer, 0, s)),
        pl.BlockSpec((None, 2, gpb, LRU_BLOCK, LRU_BLOCK), lambda b, s: (layer, 0, s, 0, 0)),
        pl.BlockSpec((None, 2, slab), lambda b, s: (layer, 0, s)),
        pl.BlockSpec((None, 2, slab), lambda b, s: (layer, 0, s)),
    ]
    r2 = D_FF // (BATCH * n_slab)
    in_specs.append(pl.BlockSpec((None, r2, D_MODEL), lambda b, s: (layer, b * n_slab + s, 0)))
    args += [conv_w, conv_b.reshape(-1, 1, LRU_WIDTH), w_a, b_a, w_i, b_i, a_logit, w_ff2]
    out_shape = [jax.ShapeDtypeStruct((N_LAT, LRU_WIDTH), BF16)]
    out_specs = [pl.BlockSpec((SEQ, slab), lambda b, s: (b, s))]
    if need_ctx:
        out_shape.append(jax.ShapeDtypeStruct((N_CTX, LRU_WIDTH), BF16))
        out_specs.append(pl.BlockSpec((CTX_LEN, slab), lambda b, s: (b, s)))
    out_shape.append(jax.ShapeDtypeStruct((D_FF, D_MODEL), BF16))
    out_specs.append(pl.BlockSpec((r2, D_MODEL), lambda b, s: (b * n_slab + s, 0)))
    return pl.pallas_call(
        functools.partial(_lru_kernel, need_ctx=need_ctx),
        out_shape=out_shape,
        grid=(BATCH, n_slab),
        in_specs=in_specs,
        out_specs=out_specs,
        scratch_shapes=[
            pltpu.VMEM((N_KEYS, slab), F32),
            pltpu.VMEM((2, N_KEYS, slab), F32),
            pltpu.VMEM((2, N_KEYS, slab), F32),
            pltpu.VMEM((N_KEYS, slab), F32),
            pltpu.VMEM((N_KEYS, slab), F32),
        ],
        compiler_params=_params(2),
        name="rglru",
    )(*args)


def _outproj_kernel(d_ref, g_ref, l_ref, w_ref, *refs, n_first):
    *x_refs, gate_ref, o_ref, wb_ref = refs

    @pl.when(pl.program_id(1) == 0)
    def _():
        wb_ref[...] = w_ref[...].astype(BF16)

    acc = jnp.dot(d_ref[...], wb_ref[0:DIFF_WIDTH, :], preferred_element_type=F32)
    acc += jnp.dot(g_ref[...], wb_ref[DIFF_WIDTH:DIFF_WIDTH + GQA_WIDTH, :], preferred_element_type=F32)
    acc += jnp.dot(l_ref[...], wb_ref[DIFF_WIDTH + GQA_WIDTH:, :], preferred_element_type=F32)
    o_ref[...] = _tile_source(pl.program_id(1), n_first, x_refs) + gate_ref[...] * acc


def _out_projection(diff, gqa, lru, w_out, xs, mods, layer, rows):
    tm, tn = 1024, 512
    n_first = N_LAT // tm
    return pl.pallas_call(
        functools.partial(_outproj_kernel, n_first=n_first),
        out_shape=jax.ShapeDtypeStruct((rows, D_MODEL), F32),
        grid=(D_MODEL // tn, rows // tm),
        in_specs=[
            pl.BlockSpec((tm, DIFF_WIDTH), lambda j, i: (i, 0)),
            pl.BlockSpec((tm, GQA_WIDTH), lambda j, i: (i, 0)),
            pl.BlockSpec((tm, LRU_WIDTH), lambda j, i: (i, 0)),
            pl.BlockSpec((None, D_MODEL, tn), lambda j, i: (layer, 0, j)),
            *_split_specs((tm, tn), n_first, 1, 0, len(xs)),
            _mod_spec(layer, 2, tm, tn, 1, 0),
        ],
        out_specs=pl.BlockSpec((tm, tn), lambda j, i: (i, j)),
        scratch_shapes=[pltpu.VMEM((D_MODEL, tn), BF16)],
        compiler_params=_params(2),
        name="out_projection",
    )(diff, gqa, lru, w_out, *xs, mods)


MLP_TM = 512
MLP_TF = 512
MLP_TN = 1024
MLP_ROWS = 64


def _mlp_kernel(x_ref, sh_ref, sc_ref, gate_ref, w1_ref, w2_ref, fin_ref, o_ref, h_ref, *, final):
    f = pl.program_id(1)

    def row_chunks(body):
        def step(r, carry):
            body(pl.ds(pl.multiple_of(r * MLP_ROWS, MLP_ROWS), MLP_ROWS))
            return carry
        lax.fori_loop(0, x_ref.shape[0] // MLP_ROWS, step, 0)

    @pl.when(f == 0)
    def _():
        def body(rows):
            y = _rms(x_ref[rows, :])
            h_ref[rows, :] = (y * (1.0 + sc_ref[...]) + sh_ref[...]).astype(BF16)
            o_ref[rows, :] = jnp.zeros((MLP_ROWS, D_MODEL), F32)
        row_chunks(body)

    t = jnp.dot(h_ref[...], w1_ref[...], preferred_element_type=F32)
    t = jnp.maximum(t, 0.0)
    tb = (t * t).astype(BF16)

    for c in range(D_MODEL // MLP_TN):
        sl = slice(c * MLP_TN, (c + 1) * MLP_TN)
        o_ref[:, sl] += jnp.dot(tb, w2_ref[:, sl], preferred_element_type=F32)

    @pl.when(f == pl.num_programs(1) - 1)
    def _():
        def body(rows):
            y = x_ref[rows, :] + gate_ref[...] * o_ref[rows, :]
            if final:
                y = _rms(y) * fin_ref[...]
            o_ref[rows, :] = y
        row_chunks(body)


def _mlp(x, mods, w1t, w2b, final_norm, layer, rows, final):
    tm, tf = MLP_TM, MLP_TF
    return pl.pallas_call(
        functools.partial(_mlp_kernel, final=final),
        out_shape=jax.ShapeDtypeStruct((rows, D_MODEL), F32),
        grid=(rows // tm, D_FF // tf),
        in_specs=[
            pl.BlockSpec((tm, D_MODEL), lambda i, f: (i, 0)),
            _mod_spec(layer, 3, tm, D_MODEL, 0, None),
            _mod_spec(layer, 4, tm, D_MODEL, 0, None),
            _mod_spec(layer, 5, tm, D_MODEL, 0, None),
            pl.BlockSpec((None, D_MODEL, tf), lambda i, f: (f, 0, 0)),
            pl.BlockSpec((tf, D_MODEL), lambda i, f: (f, 0)),
            pl.BlockSpec((1, D_MODEL), lambda i, f: (0, 0)),
        ],
        out_specs=pl.BlockSpec((tm, D_MODEL), lambda i, f: (i, 0)),
        scratch_shapes=[pltpu.VMEM((tm, D_MODEL), BF16)],
        compiler_params=_params(2),
        name="mlp_final" if final else "mlp",
    )(x, mods, mods, mods, w1t, w2b, final_norm.reshape(1, D_MODEL))


def kernel(x, c, ctx, c_ctx, w_mod, b_mod, w_in, diff_lambda, diff_subln, gqa_q_norm, gqa_k_norm,
           lru_conv_w, lru_conv_b, lru_w_a, lru_b_a, lru_w_i, lru_b_i, lru_a_logit,
           w_out, w_ff1, w_ff2, final_norm):
    depth = w_mod.shape[0]
    c8 = jnp.concatenate([c, c_ctx[None], jnp.zeros((8 - BATCH - 1, D_MODEL), F32)], axis=0)
    mods = _modulation(c8, w_mod, b_mod).reshape(depth * 8 * N_MOD, 1, D_MODEL)
    tables = _rope_tables(DIFF_QK_DIM) + _rope_tables(HEAD_DIM)

    xs = (x.reshape(N_LAT, D_MODEL), ctx.reshape(N_CTX, D_MODEL))
    for l in range(depth):
        need_ctx = l < depth - 1
        rows = N_TOK if need_ctx else N_LAT
        h = _norm_mod(xs, mods, l)
        p = _in_projection(h, w_in, l)
        qd0, qd1, kd, qg, kg = _qk_prep(p, gqa_q_norm, gqa_k_norm, tables, l)
        diff, w1t = _diff_attention(qd0, qd1, kd, p, diff_lambda, diff_subln, l, True, w_ff1)
        gqa = _gqa_attention(qg, kg, p, True)
        *lru, w2b = _rglru(p, lru_conv_w, lru_conv_b, lru_w_a, lru_b_a, lru_w_i, lru_b_i, lru_a_logit, w_ff2, l,
                           need_ctx)
        if need_ctx:
            diff = jnp.concatenate([diff, _diff_attention(qd0, qd1, kd, p, diff_lambda, diff_subln, l, False)], 0)
            gqa = jnp.concatenate([gqa, _gqa_attention(qg, kg, p, False)], 0)
            lru = jnp.concatenate(lru, 0)
        else:
            lru = lru[0]
        xt = _out_projection(diff, gqa, lru, w_out, xs, mods, l, rows)
        xs = (_mlp(xt, mods, w1t, w2b, final_norm, l, rows, not need_ctx),)
    return xs[0].reshape(BATCH, SEQ, D_MODEL)
```

```python
import functools
import math

import jax
import jax.numpy as jnp
from jax import lax
from jax.experimental import pallas as pl
from jax.experimental.pallas import tpu as pltpu

D_MODEL = 4096
BATCH = 4
SEQ = 2048
DEPTH = 2
GRID_W = 64
CTX_LEN = 256
HEAD_DIM = 128
DIFF_WIDTH = D_MODEL // 4
DIFF_HEADS = DIFF_WIDTH // HEAD_DIM
DIFF_QK_DIM = HEAD_DIM // 2
GQA_WIDTH = D_MODEL // 4
GQA_HEADS = GQA_WIDTH // HEAD_DIM
GQA_KV_HEADS = 2
GQA_GROUP = GQA_HEADS // GQA_KV_HEADS
LRU_WIDTH = D_MODEL // 2
LRU_BLOCK = 128
LRU_BLOCKS = LRU_WIDTH // LRU_BLOCK
CONV_WIDTH = 4
LRU_C = 8.0
D_FF = 4 * D_MODEL
IN_COLS = 3 * DIFF_WIDTH + GQA_WIDTH + 2 * GQA_KV_HEADS * HEAD_DIM + 2 * LRU_WIDTH
ROPE_THETA = 10000.0
EPS = 1e-6
N_MOD = 6

N_LAT = BATCH * SEQ
N_CTX = BATCH * CTX_LEN
N_TOK = N_LAT + N_CTX
N_KEYS = CTX_LEN + SEQ

COL_DQ = 0
COL_DK = COL_DQ + DIFF_WIDTH
COL_DV = COL_DK + DIFF_WIDTH
COL_GQ = COL_DV + DIFF_WIDTH
COL_GK = COL_GQ + GQA_WIDTH
COL_GV = COL_GK + GQA_KV_HEADS * HEAD_DIM
COL_RX = COL_GV + GQA_KV_HEADS * HEAD_DIM
COL_RG = COL_RX + LRU_WIDTH

LANES = 128
SUBLANES = 8
VMEM_LIMIT = 56 * 1024 * 1024

F32 = jnp.float32
BF16 = jnp.bfloat16


def _params(n_axes, vmem=VMEM_LIMIT):
    return pltpu.CompilerParams(dimension_semantics=("arbitrary",) * n_axes, vmem_limit_bytes=vmem)


def _rms(x):
    return x * lax.rsqrt(jnp.mean(x * x, axis=-1, keepdims=True) + EPS)


def _mod_row(tile, tm):
    return jnp.minimum((tile * tm) // SEQ, BATCH)


def _mod_kernel(c_ref, w_ref, b_ref, o_ref):
    c = c_ref[...]
    s = (c * jax.nn.sigmoid(c)).astype(BF16)
    o_ref[...] = jnp.dot(s, w_ref[...].astype(BF16), preferred_element_type=F32) + b_ref[...]


def _modulation(c8, w_mod, b_mod):
    depth = w_mod.shape[0]
    tn = 512
    return pl.pallas_call(
        _mod_kernel,
        out_shape=jax.ShapeDtypeStruct((depth, 8, N_MOD * D_MODEL), F32),
        grid=(depth, N_MOD * D_MODEL // tn),
        in_specs=[
            pl.BlockSpec((8, D_MODEL), lambda l, j: (0, 0)),
            pl.BlockSpec((None, D_MODEL, tn), lambda l, j: (l, 0, j)),
            pl.BlockSpec((None, 1, tn), lambda l, j: (l, 0, j)),
        ],
        out_specs=pl.BlockSpec((None, 8, tn), lambda l, j: (l, 0, j)),
        compiler_params=_params(2),
        name="modulation",
    )(c8, w_mod, b_mod.reshape(depth, 1, N_MOD * D_MODEL))


def _mod_spec(layer, which, tm, tn, row_axis, col_axis):
    def index_map(*ids):
        row = _mod_row(ids[row_axis], tm)
        col = 0 if col_axis is None else ids[col_axis]
        return (layer * 8 * N_MOD + row * N_MOD + which, 0, col)
    return pl.BlockSpec((None, 1, tn), index_map)


def _tile_source(i, n_first, refs):
    if len(refs) == 1:
        return refs[0][...]
    return jnp.where(i < n_first, refs[0][...], refs[1][...])


def _split_specs(block, n_first, tile_axis, col_axis, n_sources):
    def col(ids):
        return 0 if col_axis is None else ids[col_axis]
    if n_sources == 1:
        return [pl.BlockSpec(block, lambda *ids: (ids[tile_axis], col(ids)))]
    return [
        pl.BlockSpec(block, lambda *ids: (jnp.minimum(ids[tile_axis], n_first - 1), col(ids))),
        pl.BlockSpec(block, lambda *ids: (jnp.maximum(ids[tile_axis] - n_first, 0), col(ids))),
    ]


def _norm_mod_kernel(*refs, n_first):
    *x_refs, sh_ref, sc_ref, o_ref = refs
    y = _rms(_tile_source(pl.program_id(0), n_first, x_refs))
    o_ref[...] = (y * (1.0 + sc_ref[...]) + sh_ref[...]).astype(BF16)


def _norm_mod(xs, mods, layer):
    tm = 256
    n_first = N_LAT // tm
    return pl.pallas_call(
        functools.partial(_norm_mod_kernel, n_first=n_first),
        out_shape=jax.ShapeDtypeStruct((N_TOK, D_MODEL), BF16),
        grid=(N_TOK // tm,),
        in_specs=[
            *_split_specs((tm, D_MODEL), n_first, 0, None, len(xs)),
            _mod_spec(layer, 0, tm, D_MODEL, 0, None),
            _mod_spec(layer, 1, tm, D_MODEL, 0, None),
        ],
        out_specs=pl.BlockSpec((tm, D_MODEL), lambda i: (i, 0)),
        compiler_params=_params(1),
        name="norm_mod",
    )(*xs, mods, mods)


def _inproj_kernel(h_ref, w_ref, o_ref, wb_ref):
    @pl.when(pl.program_id(1) == 0)
    def _():
        wb_ref[...] = w_ref[...].astype(BF16)

    o_ref[...] = jnp.dot(h_ref[...], wb_ref[...], preferred_element_type=F32).astype(o_ref.dtype)


def _in_projection(h, w_in, layer):
    tm, tn = 1536, 512
    return pl.pallas_call(
        _inproj_kernel,
        out_shape=jax.ShapeDtypeStruct((N_TOK, IN_COLS), BF16),
        grid=(IN_COLS // tn, N_TOK // tm),
        in_specs=[
            pl.BlockSpec((tm, D_MODEL), lambda j, i: (i, 0)),
            pl.BlockSpec((None, D_MODEL, tn), lambda j, i: (layer, 0, j)),
        ],
        out_specs=pl.BlockSpec((tm, tn), lambda j, i: (i, j)),
        scratch_shapes=[pltpu.VMEM((D_MODEL, tn), BF16)],
        compiler_params=_params(2),
        name="in_projection",
    )(h, w_in)


def _rope_tables(dim):
    n_freq = dim // 4
    pos = jnp.arange(SEQ)
    row = (pos // GRID_W).astype(F32)
    col = (pos % GRID_W).astype(F32)
    inv = ROPE_THETA ** (-jnp.arange(n_freq, dtype=F32) / n_freq)
    ang_r = row[:, None] * inv
    ang_c = col[:, None] * inv
    cos = jnp.concatenate([jnp.cos(ang_r), jnp.cos(ang_r), jnp.cos(ang_c), jnp.cos(ang_c)], axis=1)
    sin = jnp.concatenate([-jnp.sin(ang_r), jnp.sin(ang_r), -jnp.sin(ang_c), jnp.sin(ang_c)], axis=1)
    reps = LANES // dim
    cos = jnp.tile(cos, (1, reps))
    sin = jnp.tile(sin, (1, reps))
    cos = jnp.concatenate([cos, jnp.ones((CTX_LEN, LANES), F32)], axis=0)
    sin = jnp.concatenate([sin, jnp.zeros((CTX_LEN, LANES), F32)], axis=0)
    return cos, sin


def _rope(x, cos, sin, first, half):
    partner = jnp.where(first, pltpu.roll(x, LANES - half, 1), pltpu.roll(x, half, 1))
    return x * cos + partner * sin


def _qkprep_kernel(dq_ref, dk_ref, gq_ref, gk_ref, cd_ref, sd_ref, cg_ref, sg_ref, qgain_ref, kgain_ref,
                   qd0_ref, qd1_ref, kd_ref, qg_ref, kg_ref):
    cd, sd, cg, sg = cd_ref[...], sd_ref[...], cg_ref[...], sg_ref[...]
    qgain, kgain = qgain_ref[...], kgain_ref[...]
    lane = lax.broadcasted_iota(jnp.int32, cd.shape, 1)
    first_d = (lane & (DIFF_QK_DIM // 2 - 1)) < DIFF_QK_DIM // 4
    first_g = (lane & (HEAD_DIM // 2 - 1)) < HEAD_DIM // 4
    map0 = lane < DIFF_QK_DIM
    d_scale = DIFF_QK_DIM ** -0.5 * LOG2E
    g_scale = HEAD_DIM ** -0.5 * LOG2E
    for c in range(DIFF_WIDTH // LANES):
        sl = slice(c * LANES, (c + 1) * LANES)
        q = _rope(dq_ref[:, sl].astype(F32), cd, sd, first_d, DIFF_QK_DIM // 4) * d_scale
        qd0_ref[:, sl] = jnp.where(map0, q, 0.0).astype(BF16)
        qd1_ref[:, sl] = jnp.where(map0, 0.0, q).astype(BF16)
        kd_ref[:, sl] = _rope(dk_ref[:, sl].astype(F32), cd, sd, first_d, DIFF_QK_DIM // 4).astype(BF16)
    for c in range(GQA_WIDTH // LANES):
        sl = slice(c * LANES, (c + 1) * LANES)
        qn = _rms(gq_ref[:, sl].astype(F32)) * qgain
        qg_ref[:, sl] = (_rope(qn, cg, sg, first_g, HEAD_DIM // 4) * g_scale).astype(BF16)
    for c in range(GQA_KV_HEADS):
        sl = slice(c * LANES, (c + 1) * LANES)
        kn = _rms(gk_ref[:, sl].astype(F32)) * kgain
        kg_ref[:, sl] = _rope(kn, cg, sg, first_g, HEAD_DIM // 4).astype(BF16)


def _qk_prep(p, q_gain, k_gain, tables, layer):
    tm = 256
    cos_d, sin_d, cos_g, sin_g = tables
    lat_tiles = N_LAT // tm

    def pos_block(i):
        return jnp.where(i < lat_tiles, i % (SEQ // tm), SEQ // tm)

    kv_w = GQA_KV_HEADS * HEAD_DIM
    tab_spec = pl.BlockSpec((tm, LANES), lambda i: (pos_block(i), 0))
    gain_spec = pl.BlockSpec((None, 1, HEAD_DIM), lambda i: (layer, 0, 0))
    wide = lambda col: pl.BlockSpec((tm, DIFF_WIDTH), lambda i: (i, col // DIFF_WIDTH))
    return pl.pallas_call(
        _qkprep_kernel,
        out_shape=[
            jax.ShapeDtypeStruct((N_TOK, DIFF_WIDTH), BF16),
            jax.ShapeDtypeStruct((N_TOK, DIFF_WIDTH), BF16),
            jax.ShapeDtypeStruct((N_TOK, DIFF_WIDTH), BF16),
            jax.ShapeDtypeStruct((N_TOK, GQA_WIDTH), BF16),
            jax.ShapeDtypeStruct((N_TOK, kv_w), BF16),
        ],
        grid=(N_TOK // tm,),
        in_specs=[
            wide(COL_DQ), wide(COL_DK), wide(COL_GQ),
            pl.BlockSpec((tm, kv_w), lambda i: (i, COL_GK // kv_w)),
            tab_spec, tab_spec, tab_spec, tab_spec, gain_spec, gain_spec,
        ],
        out_specs=[
            pl.BlockSpec((tm, DIFF_WIDTH), lambda i: (i, 0)),
            pl.BlockSpec((tm, DIFF_WIDTH), lambda i: (i, 0)),
            pl.BlockSpec((tm, DIFF_WIDTH), lambda i: (i, 0)),
            pl.BlockSpec((tm, GQA_WIDTH), lambda i: (i, 0)),
            pl.BlockSpec((tm, kv_w), lambda i: (i, 0)),
        ],
        compiler_params=_params(1),
        name="qk_prep",
    )(p, p, p, p, cos_d, sin_d, cos_g, sin_g, q_gain.reshape(-1, 1, HEAD_DIM), k_gain.reshape(-1, 1, HEAD_DIM))


ATT_ROWS = 512
ATT_KCHUNK = 768
LOG2E = 1.4426950408889634


def _assemble_kv(k_refs, v_refs, k_all, v_aug):
    r = 0
    for k, v in zip(k_refs, v_refs):
        n = k.shape[0]
        k_all[r:r + n, :] = k[...]
        v_aug[r:r + n, 0:HEAD_DIM] = v[...]
        r += n
    v_aug[:, HEAD_DIM:] = jnp.ones((v_aug.shape[0], HEAD_DIM), BF16)


def _tile_scores(qq, k_all, s_ref, m_ref):
    s = lax.dot_general(qq, k_all[...], (((1,), (1,)), ((), ())), preferred_element_type=F32)
    s_ref[...] = s
    m_ref[...] = jnp.max(s, axis=-1, keepdims=True)


def _tile_pv(s_ref, m_ref, v_aug):
    m = m_ref[...]
    n_keys = s_ref.shape[1]
    chunk = min(ATT_KCHUNK, n_keys)
    acc = None
    for c in range(0, n_keys, chunk):
        e = jnp.exp2(s_ref[:, c:c + chunk] - m).astype(BF16)
        o = jnp.dot(e, v_aug[c:c + chunk, :], preferred_element_type=F32)
        acc = o if acc is None else acc + o
    return acc[:, :HEAD_DIM] / acc[:, HEAD_DIM:]


def _attn_pipeline(n_tiles, load_q, finish, k_all, v_aug, sa, ma, sb, mb):
    _tile_scores(load_q(0), k_all, sa, ma)
    if n_tiles == 1:
        finish(0, _tile_pv(sa, ma, v_aug))
        return
    assert n_tiles % 2 == 0

    def pair(j, carry):
        t0 = 2 * j
        r0 = _tile_pv(sa, ma, v_aug)
        _tile_scores(load_q(t0 + 1), k_all, sb, mb)
        finish(t0, r0)
        r1 = _tile_pv(sb, mb, v_aug)
        _tile_scores(load_q(jnp.minimum(t0 + 2, n_tiles - 1)), k_all, sa, ma)
        finish(t0 + 1, r1)
        return carry

    lax.fori_loop(0, n_tiles // 2, pair, 0)


def _att_scratch(n_keys):
    return [
        pltpu.VMEM((n_keys, HEAD_DIM), BF16),
        pltpu.VMEM((n_keys, 2 * HEAD_DIM), BF16),
        pltpu.VMEM((ATT_ROWS, n_keys), F32),
        pltpu.VMEM((ATT_ROWS, 1), F32),
        pltpu.VMEM((ATT_ROWS, n_keys), F32),
        pltpu.VMEM((ATT_ROWS, 1), F32),
    ]


def _tile_rows(t, tq):
    return pl.ds(t * tq if isinstance(t, int) else pl.multiple_of(t * tq, tq), tq)


def _diff_attn_kernel(lam_ref, gain_ref, q0_ref, q1_ref, *refs, lam_init, n_blocks, cast):
    k_refs = refs[0:2 * n_blocks:2]
    v_refs = refs[1:2 * n_blocks:2]
    refs = refs[2 * n_blocks:]
    if cast:
        f1_ref, o_ref, c1_ref = refs[:3]
        refs = refs[3:]
        for c in range(D_FF // MLP_TF):
            c1_ref[c] = f1_ref[:, c * MLP_TF:(c + 1) * MLP_TF].astype(BF16)
    else:
        o_ref = refs[0]
        refs = refs[1:]
    k_all, v_aug, sa, ma, sb, mb = refs
    tq = ATT_ROWS // 2
    n_tiles = q0_ref.shape[0] // tq
    lv = lam_ref[...]
    lam = (jnp.exp(jnp.sum(lv[0:1] * lv[1:2], axis=-1, keepdims=True))
           - jnp.exp(jnp.sum(lv[2:3] * lv[3:4], axis=-1, keepdims=True)) + lam_init)
    gain = gain_ref[...] * (1.0 - lam_init)
    _assemble_kv(k_refs, v_refs, k_all, v_aug)

    def load_q(t):
        return jnp.concatenate([q0_ref[_tile_rows(t, tq), :], q1_ref[_tile_rows(t, tq), :]], axis=0)

    def finish(t, r):
        o = r[:tq] - lam * r[tq:]
        o_ref[_tile_rows(t, tq), :] = (_rms(o) * gain).astype(o_ref.dtype)

    _attn_pipeline(n_tiles, load_q, finish, k_all, v_aug, sa, ma, sb, mb)


def _diff_attention(qd0, qd1, kd, p, lam_vecs, subln, layer_idx, latent, w_ff1=None):
    lam_init = 0.8 - 0.6 * math.exp(-0.3 * layer_idx)
    cast = w_ff1 is not None
    v_col = COL_DV // HEAD_DIM
    ctx_blk = N_LAT // CTX_LEN
    kv_specs = [
        pl.BlockSpec((CTX_LEN, HEAD_DIM), lambda b, h: (ctx_blk + b, h)),
        pl.BlockSpec((CTX_LEN, HEAD_DIM), lambda b, h: (ctx_blk + b, v_col + h)),
    ]
    kv_args = (kd, p)
    if latent:
        rows, n_q, n_keys = N_LAT, SEQ, N_KEYS
        q_spec = pl.BlockSpec((SEQ, HEAD_DIM), lambda b, h: (b, h))
        kv_specs += [
            pl.BlockSpec((SEQ, HEAD_DIM), lambda b, h: (b, h)),
            pl.BlockSpec((SEQ, HEAD_DIM), lambda b, h: (b, v_col + h)),
        ]
        kv_args += (kd, p)
    else:
        rows, n_q, n_keys = N_CTX, CTX_LEN, CTX_LEN
        q_spec = pl.BlockSpec((CTX_LEN, HEAD_DIM), lambda b, h: (ctx_blk + b, h))
    out_shape = [jax.ShapeDtypeStruct((rows, DIFF_WIDTH), BF16)]
    out_specs = [pl.BlockSpec((n_q, HEAD_DIM), lambda b, h: (b, h))]
    if cast:
        r1 = D_MODEL // (BATCH * DIFF_HEADS)
        kv_specs.append(pl.BlockSpec((None, r1, D_FF), lambda b, h: (layer_idx, b * DIFF_HEADS + h, 0)))
        kv_args += (w_ff1,)
        out_shape.append(jax.ShapeDtypeStruct((D_FF // MLP_TF, D_MODEL, MLP_TF), BF16))
        out_specs.append(pl.BlockSpec((D_FF // MLP_TF, r1, MLP_TF), lambda b, h: (0, b * DIFF_HEADS + h, 0)))
    out = pl.pallas_call(
        functools.partial(_diff_attn_kernel, lam_init=lam_init, n_blocks=(len(kv_args) - cast) // 2, cast=cast),
        out_shape=out_shape,
        grid=(BATCH, DIFF_HEADS),
        in_specs=[
            pl.BlockSpec((None, 4, DIFF_QK_DIM), lambda b, h: (layer_idx, 0, 0)),
            pl.BlockSpec((None, 1, HEAD_DIM), lambda b, h: (layer_idx, 0, 0)),
            q_spec, q_spec, *kv_specs,
        ],
        out_specs=out_specs,
        scratch_shapes=_att_scratch(n_keys),
        compiler_params=_params(2),
        name="diff_attention_lat" if latent else "diff_attention_ctx",
    )(lam_vecs, subln.reshape(-1, 1, HEAD_DIM), qd0, qd1, *kv_args)
    return tuple(out) if cast else out[0]


def _gqa_attn_kernel(q_ref, *refs, n_blocks):
    k_refs = refs[0:2 * n_blocks:2]
    v_refs = refs[1:2 * n_blocks:2]
    o_ref = refs[2 * n_blocks]
    k_all, v_aug, sa, ma, sb, mb = refs[2 * n_blocks + 1:]
    tq = ATT_ROWS // GQA_GROUP
    n_tiles = q_ref.shape[0] // tq
    _assemble_kv(k_refs, v_refs, k_all, v_aug)

    def load_q(t):
        rows = _tile_rows(t, tq)
        return jnp.concatenate([q_ref[rows, g * HEAD_DIM:(g + 1) * HEAD_DIM] for g in range(GQA_GROUP)], axis=0)

    def finish(t, r):
        rows = _tile_rows(t, tq)
        for g in range(GQA_GROUP):
            o_ref[rows, g * HEAD_DIM:(g + 1) * HEAD_DIM] = r[g * tq:(g + 1) * tq].astype(o_ref.dtype)

    _attn_pipeline(n_tiles, load_q, finish, k_all, v_aug, sa, ma, sb, mb)


def _gqa_attention(qg, kg, p, latent):
    v_col = COL_GV // HEAD_DIM
    ctx_blk = N_LAT // CTX_LEN
    gw = GQA_GROUP * HEAD_DIM
    kv_specs = [
        pl.BlockSpec((CTX_LEN, HEAD_DIM), lambda b, g: (ctx_blk + b, g)),
        pl.BlockSpec((CTX_LEN, HEAD_DIM), lambda b, g: (ctx_blk + b, v_col + g)),
    ]
    kv_args = (kg, p)
    if latent:
        rows, n_q, n_keys = N_LAT, SEQ, N_KEYS
        q_spec = pl.BlockSpec((SEQ, gw), lambda b, g: (b, g))
        kv_specs += [
            pl.BlockSpec((SEQ, HEAD_DIM), lambda b, g: (b, g)),
            pl.BlockSpec((SEQ, HEAD_DIM), lambda b, g: (b, v_col + g)),
        ]
        kv_args += (kg, p)
    else:
        rows, n_q, n_keys = N_CTX, CTX_LEN, CTX_LEN
        q_spec = pl.BlockSpec((CTX_LEN, gw), lambda b, g: (ctx_blk + b, g))
    return pl.pallas_call(
        functools.partial(_gqa_attn_kernel, n_blocks=len(kv_args) // 2),
        out_shape=jax.ShapeDtypeStruct((rows, GQA_WIDTH), BF16),
        grid=(BATCH, GQA_KV_HEADS),
        in_specs=[q_spec, *kv_specs],
        out_specs=pl.BlockSpec((n_q, gw), lambda b, g: (b, g)),
        scratch_shapes=_att_scratch(n_keys),
        compiler_params=_params(2),
        name="gqa_attention_lat" if latent else "gqa_attention_ctx",
    )(qg, *kv_args)


LRU_SLAB = 256
LRU_ROWS = 256


def _conv_centred(x, w, bias):
    n = x.shape[0]
    row = lax.broadcasted_iota(jnp.int32, x.shape, 0)
    xm1 = jnp.where(row >= 1, pltpu.roll(x, 1, 0), 0.0)
    xp1 = jnp.where(row < n - 1, pltpu.roll(x, n - 1, 0), 0.0)
    xp2 = jnp.where(row < n - 2, pltpu.roll(x, n - 2, 0), 0.0)
    return xm1 * w[0:1] + x * w[1:2] + xp1 * w[2:3] + xp2 * w[3:4] + bias


def _gelu_tanh(g):
    return 0.5 * g * (1.0 + jnp.tanh(0.7978845608028654 * (g + 0.044715 * (g * g * g))))


def _lru_kernel(*refs, need_ctx):
    if need_ctx:
        (xl_ref, xc_ref, gl_ref, gc_ref, cw_ref, cb_ref, wa_ref, ba_ref, wi_ref, bi_ref, al_ref, f2_ref,
         yl_ref, yc_ref, c2_ref, xconv_ref, a_ref, u_ref, hf_ref, hb_ref) = refs
    else:
        (xl_ref, xc_ref, gl_ref, cw_ref, cb_ref, wa_ref, ba_ref, wi_ref, bi_ref, al_ref, f2_ref,
         yl_ref, c2_ref, xconv_ref, a_ref, u_ref, hf_ref, hb_ref) = refs

    def cast_rows(r, carry):
        rows = pl.ds(pl.multiple_of(r * LRU_ROWS, LRU_ROWS), LRU_ROWS)
        c2_ref[rows, :] = f2_ref[rows, :].astype(BF16)
        return carry

    lax.fori_loop(0, f2_ref.shape[0] // LRU_ROWS, cast_rows, 0)

    cw = cw_ref[...]
    cb = cb_ref[...]
    xconv_ref[0:CTX_LEN, :] = _conv_centred(xc_ref[...].astype(F32), cw, cb)
    xconv_ref[CTX_LEN:N_KEYS, :] = _conv_centred(xl_ref[...].astype(F32), cw, cb)

    z = -al_ref[...]
    sp = jnp.maximum(z, 0.0) + jnp.log1p(jnp.exp(-jnp.abs(z)))
    rate = (-LRU_C * LOG2E) * sp
    ba = ba_ref[...]
    bi = bi_ref[...]

    def gate_chunk(ci, carry):
        r0 = pl.multiple_of(ci * LRU_ROWS, LRU_ROWS)
        xc = xconv_ref[pl.ds(r0, LRU_ROWS), :]
        xb = xc.astype(BF16)
        for g in range(LRU_SLAB // LRU_BLOCK):
            sl = slice(g * LRU_BLOCK, (g + 1) * LRU_BLOCK)
            for d in range(2):
                ra = jnp.dot(xb[:, sl], wa_ref[d, g].astype(BF16), preferred_element_type=F32) + ba[d:d + 1, sl]
                ri = jnp.dot(xb[:, sl], wi_ref[d, g].astype(BF16), preferred_element_type=F32) + bi[d:d + 1, sl]
                a = jnp.exp2((0.5 * jnp.tanh(0.5 * ra) + 0.5) * rate[d:d + 1, sl])
                u = jnp.sqrt(1.0 - a * a) * ((0.5 * jnp.tanh(0.5 * ri) + 0.5) * xc[:, sl])
                a_ref[d, pl.ds(r0, LRU_ROWS), sl] = a
                u_ref[d, pl.ds(r0, LRU_ROWS), sl] = u
        return carry

    lax.fori_loop(0, N_KEYS // LRU_ROWS, gate_chunk, 0)

    row8 = lax.broadcasted_iota(jnp.int32, (SUBLANES, LRU_SLAB), 0)

    def block_scan(a, u, h_in, reverse):
        for k in (1, 2, 4):
            if reverse:
                keep = row8 < SUBLANES - k
                shift = SUBLANES - k
            else:
                keep = row8 >= k
                shift = k
            a_s = jnp.where(keep, pltpu.roll(a, shift, 0), 1.0)
            u_s = jnp.where(keep, pltpu.roll(u, shift, 0), 0.0)
            u = a * u_s + u
            a = a * a_s
        h = a * h_in + u
        last = h[0:1, :] if reverse else h[SUBLANES - 1:SUBLANES, :]
        return h, jnp.broadcast_to(last, h.shape)

    def scan_pair(base, n_blocks):
        def body(i, carry):
            hf, hb = carry
            rf = pl.multiple_of(base + i * SUBLANES, SUBLANES)
            rb = pl.multiple_of(base + (n_blocks - 1 - i) * SUBLANES, SUBLANES)
            h, hf = block_scan(a_ref[0, pl.ds(rf, SUBLANES), :], u_ref[0, pl.ds(rf, SUBLANES), :], hf, False)
            hf_ref[pl.ds(rf, SUBLANES), :] = h
            h, hb = block_scan(a_ref[1, pl.ds(rb, SUBLANES), :], u_ref[1, pl.ds(rb, SUBLANES), :], hb, True)
            hb_ref[pl.ds(rb, SUBLANES), :] = h
            return hf, hb
        return body

    zero = jnp.zeros((SUBLANES, LRU_SLAB), F32)
    carry = lax.fori_loop(0, CTX_LEN // SUBLANES, scan_pair(0, CTX_LEN // SUBLANES), (zero, zero), unroll=2)
    lax.fori_loop(0, SEQ // SUBLANES, scan_pair(CTX_LEN, SEQ // SUBLANES), carry, unroll=2)

    def out_chunk(ci, carry):
        r0 = pl.multiple_of(ci * LRU_ROWS, LRU_ROWS)
        h = hf_ref[pl.ds(CTX_LEN + r0, LRU_ROWS), :] + hb_ref[pl.ds(CTX_LEN + r0, LRU_ROWS), :]
        yl_ref[pl.ds(r0, LRU_ROWS), :] = (h * _gelu_tanh(gl_ref[pl.ds(r0, LRU_ROWS), :].astype(F32))).astype(BF16)
        return carry

    lax.fori_loop(0, SEQ // LRU_ROWS, out_chunk, 0)
    if need_ctx:
        h = hf_ref[0:CTX_LEN, :] + hb_ref[0:CTX_LEN, :]
        yc_ref[...] = (h * _gelu_tanh(gc_ref[...].astype(F32))).astype(BF16)


def _rglru(p, conv_w, conv_b, w_a, b_a, w_i, b_i, a_logit, w_ff2, layer, need_ctx):
    slab = LRU_SLAB
    n_slab = LRU_WIDTH // slab
    gpb = slab // LRU_BLOCK
    ctx_blk = N_LAT // CTX_LEN
    x_col = COL_RX // slab
    g_col = COL_RG // slab
    in_specs = [
        pl.BlockSpec((SEQ, slab), lambda b, s: (b, x_col + s)),
        pl.BlockSpec((CTX_LEN, slab), lambda b, s: (ctx_blk + b, x_col + s)),
        pl.BlockSpec((SEQ, slab), lambda b, s: (b, g_col + s)),
    ]
    args = [p, p, p]
    if need_ctx:
        in_specs.append(pl.BlockSpec((CTX_LEN, slab), lambda b, s: (ctx_blk + b, g_col + s)))
        args.append(p)
    in_specs += [
        pl.BlockSpec((None, CONV_WIDTH, slab), lambda b, s: (layer, 0, s)),
        pl.BlockSpec((None, 1, slab), lambda b, s: (layer, 0, s)),
        pl.BlockSpec((None, 2, gpb, LRU_BLOCK, LRU_BLOCK), lambda b, s: (layer, 0, s, 0, 0)),
        pl.BlockSpec((None, 2, slab), lambda b, s: (layer, 0, s)),
        pl.BlockSpec((None, 2, gpb, LRU_BLOCK, LRU_BLOCK), lambda b, s: (layer, 0, s, 0, 0)),
        pl.BlockSpec((None, 2, slab), lambda b, s: (layer, 0, s)),
        pl.BlockSpec((None, 2, slab), lambda b, s: (layer, 0, s)),
    ]
    r2 = D_FF // (BATCH * n_slab)
    in_specs.append(pl.BlockSpec((None, r2, D_MODEL), lambda b, s: (layer, b * n_slab + s, 0)))
    args += [conv_w, conv_b.reshape(-1, 1, LRU_WIDTH), w_a, b_a, w_i, b_i, a_logit, w_ff2]
    out_shape = [jax.ShapeDtypeStruct((N_LAT, LRU_WIDTH), BF16)]
    out_specs = [pl.BlockSpec((SEQ, slab), lambda b, s: (b, s))]
    if need_ctx:
        out_shape.append(jax.ShapeDtypeStruct((N_CTX, LRU_WIDTH), BF16))
        out_specs.append(pl.BlockSpec((CTX_LEN, slab), lambda b, s: (b, s)))
    out_shape.append(jax.ShapeDtypeStruct((D_FF, D_MODEL), BF16))
    out_specs.append(pl.BlockSpec((r2, D_MODEL), lambda b, s: (b * n_slab + s, 0)))
    return pl.pallas_call(
        functools.partial(_lru_kernel, need_ctx=need_ctx),
        out_shape=out_shape,
        grid=(BATCH, n_slab),
        in_specs=in_specs,
        out_specs=out_specs,
        scratch_shapes=[
            pltpu.VMEM((N_KEYS, slab), F32),
            pltpu.VMEM((2, N_KEYS, slab), F32),
            pltpu.VMEM((2, N_KEYS, slab), F32),
            pltpu.VMEM((N_KEYS, slab), F32),
            pltpu.VMEM((N_KEYS, slab), F32),
        ],
        compiler_params=_params(2),
        name="rglru",
    )(*args)


def _outproj_kernel(d_ref, g_ref, l_ref, w_ref, *refs, n_first):
    *x_refs, gate_ref, o_ref, wb_ref = refs

    @pl.when(pl.program_id(1) == 0)
    def _():
        wb_ref[...] = w_ref[...].astype(BF16)

    acc = jnp.dot(d_ref[...], wb_ref[0:DIFF_WIDTH, :], preferred_element_type=F32)
    acc += jnp.dot(g_ref[...], wb_ref[DIFF_WIDTH:DIFF_WIDTH + GQA_WIDTH, :], preferred_element_type=F32)
    acc += jnp.dot(l_ref[...], wb_ref[DIFF_WIDTH + GQA_WIDTH:, :], preferred_element_type=F32)
    o_ref[...] = _tile_source(pl.program_id(1), n_first, x_refs) + gate_ref[...] * acc


def _out_projection(diff, gqa, lru, w_out, xs, mods, layer, rows):
    tm, tn = 1024, 512
    n_first = N_LAT // tm
    return pl.pallas_call(
        functools.partial(_outproj_kernel, n_first=n_first),
        out_shape=jax.ShapeDtypeStruct((rows, D_MODEL), F32),
        grid=(D_MODEL // tn, rows // tm),
        in_specs=[
            pl.BlockSpec((tm, DIFF_WIDTH), lambda j, i: (i, 0)),
            pl.BlockSpec((tm, GQA_WIDTH), lambda j, i: (i, 0)),
            pl.BlockSpec((tm, LRU_WIDTH), lambda j, i: (i, 0)),
            pl.BlockSpec((None, D_MODEL, tn), lambda j, i: (layer, 0, j)),
            *_split_specs((tm, tn), n_first, 1, 0, len(xs)),
            _mod_spec(layer, 2, tm, tn, 1, 0),
        ],
        out_specs=pl.BlockSpec((tm, tn), lambda j, i: (i, j)),
        scratch_shapes=[pltpu.VMEM((D_MODEL, tn), BF16)],
        compiler_params=_params(2),
        name="out_projection",
    )(diff, gqa, lru, w_out, *xs, mods)


MLP_TM = 512
MLP_TF = 512
MLP_TN = 1024
MLP_ROWS = 64


def _mlp_kernel(x_ref, sh_ref, sc_ref, gate_ref, w1_ref, w2_ref, fin_ref, o_ref, h_ref, *, final):
    f = pl.program_id(1)

    def row_chunks(body):
        def step(r, carry):
            body(pl.ds(pl.multiple_of(r * MLP_ROWS, MLP_ROWS), MLP_ROWS))
            return carry
        lax.fori_loop(0, x_ref.shape[0] // MLP_ROWS, step, 0)

    @pl.when(f == 0)
    def _():
        def body(rows):
            y = _rms(x_ref[rows, :])
            h_ref[rows, :] = (y * (1.0 + sc_ref[...]) + sh_ref[...]).astype(BF16)
            o_ref[rows, :] = jnp.zeros((MLP_ROWS, D_MODEL), F32)
        row_chunks(body)

    t = jnp.dot(h_ref[...], w1_ref[...], preferred_element_type=F32)
    t = jnp.maximum(t, 0.0)
    tb = (t * t).astype(BF16)

    for c in range(D_MODEL // MLP_TN):
        sl = slice(c * MLP_TN, (c + 1) * MLP_TN)
        o_ref[:, sl] += jnp.dot(tb, w2_ref[:, sl], preferred_element_type=F32)

    @pl.when(f == pl.num_programs(1) - 1)
    def _():
        def body(rows):
            y = x_ref[rows, :] + gate_ref[...] * o_ref[rows, :]
            if final:
                y = _rms(y) * fin_ref[...]
            o_ref[rows, :] = y
        row_chunks(body)


def _mlp(x, mods, w1t, w2b, final_norm, layer, rows, final):
    tm, tf = MLP_TM, MLP_TF
    return pl.pallas_call(
        functools.partial(_mlp_kernel, final=final),
        out_shape=jax.ShapeDtypeStruct((rows, D_MODEL), F32),
        grid=(rows // tm, D_FF // tf),
        in_specs=[
            pl.BlockSpec((tm, D_MODEL), lambda i, f: (i, 0)),
            _mod_spec(layer, 3, tm, D_MODEL, 0, None),
            _mod_spec(layer, 4, tm, D_MODEL, 0, None),
            _mod_spec(layer, 5, tm, D_MODEL, 0, None),
            pl.BlockSpec((None, D_MODEL, tf), lambda i, f: (f, 0, 0)),
            pl.BlockSpec((tf, D_MODEL), lambda i, f: (f, 0)),
            pl.BlockSpec((1, D_MODEL), lambda i, f: (0, 0)),
        ],
        out_specs=pl.BlockSpec((tm, D_MODEL), lambda i, f: (i, 0)),
        scratch_shapes=[pltpu.VMEM((tm, D_MODEL), BF16)],
        compiler_params=_params(2),
        name="mlp_final" if final else "mlp",
    )(x, mods, mods, mods, w1t, w2b, final_norm.reshape(1, D_MODEL))


def kernel(x, c, ctx, c_ctx, w_mod, b_mod, w_in, diff_lambda, diff_subln, gqa_q_norm, gqa_k_norm,
           lru_conv_w, lru_conv_b, lru_w_a, lru_b_a, lru_w_i, lru_b_i, lru_a_logit,
           w_out, w_ff1, w_ff2, final_norm):
    depth = w_mod.shape[0]
    c8 = jnp.concatenate([c, c_ctx[None], jnp.zeros((8 - BATCH - 1, D_MODEL), F32)], axis=0)
    mods = _modulation(c8, w_mod, b_mod).reshape(depth * 8 * N_MOD, 1, D_MODEL)
    tables = _rope_tables(DIFF_QK_DIM) + _rope_tables(HEAD_DIM)

    xs = (x.reshape(N_LAT, D_MODEL), ctx.reshape(N_CTX, D_MODEL))
    for l in range(depth):
        need_ctx = l < depth - 1
        rows = N_TOK if need_ctx else N_LAT
        h = _norm_mod(xs, mods, l)
        p = _in_projection(h, w_in, l)
        qd0, qd1, kd, qg, kg = _qk_prep(p, gqa_q_norm, gqa_k_norm, tables, l)
        diff, w1t = _diff_attention(qd0, qd1, kd, p, diff_lambda, diff_subln, l, True, w_ff1)
        gqa = _gqa_attention(qg, kg, p, True)
        *lru, w2b = _rglru(p, lru_conv_w, lru_conv_b, lru_w_a, lru_b_a, lru_w_i, lru_b_i, lru_a_logit, w_ff2, l,
                           need_ctx)
        if need_ctx:
            diff = jnp.concatenate([diff, _diff_attention(qd0, qd1, kd, p, diff_lambda, diff_subln, l, False)], 0)
            gqa = jnp.concatenate([gqa, _gqa_attention(qg, kg, p, False)], 0)
            lru = jnp.concatenate(lru, 0)
        else:
            lru = lru[0]
        xt = _out_projection(diff, gqa, lru, w_out, xs, mods, l, rows)
        xs = (_mlp(xt, mods, w1t, w2b, final_norm, l, rows, not need_ctx),)
    return xs[0].reshape(BATCH, SEQ, D_MODEL)
```
